```python
import jax, jax.numpy as jnp
from jax import lax
import numpy as np

D_MODEL = 1024
BATCH = 4
SEQ = 4096
DEPTH = 2
DEC_BATCH = 32
DEC_SEQ = 8
PAST_LEN = 8192
PAGE_SIZE = 128

HEAD_DIM = 64
MIX_WIDTH = D_MODEL
GROUP_WIDTH = MIX_WIDTH // 4
N_GROUP_HEADS = GROUP_WIDTH // HEAD_DIM
H_RWKV = N_GROUP_HEADS
H_FOX = N_GROUP_HEADS
H_MOBA = N_GROUP_HEADS
H_SB = N_GROUP_HEADS
H_ATT = H_FOX + H_MOBA + H_SB
RWKV_DECAY_RANK = 64
RWKV_A_RANK = 64
RWKV_GATE_RANK = 128
N_RWKV_COLS = 3 * GROUP_WIDTH + RWKV_DECAY_RANK + RWKV_A_RANK + RWKV_GATE_RANK
N_IN = N_RWKV_COLS + 3 * H_ATT * HEAD_DIM + H_FOX
D_FF = 4 * D_MODEL
MOBA_BLOCK = 256
MOBA_TOPK = 3
Q_BLOCK = 128
MOBA_Q_BLOCK = 64
RMS_EPS = 1e-6
RWKV_GN_EPS = 64e-5
RWKV_DECAY_OFFSET = 0.5
FORGET_BIAS = 3.0
POOL_NUM = 5
POOL_DEN = 4

kernel_name = 'hybrid_rwkv7_fox_moba_stickbreaking_step'


def rms_norm(x, g):
    xf = x.astype(jnp.float32)
    y = xf * lax.rsqrt(jnp.mean(xf * xf, axis=-1, keepdims=True) + RMS_EPS)
    return (y * g.astype(jnp.float32)).astype(x.dtype)


def _blocks(a, blk):
    b, t = a.shape[:2]
    return jnp.moveaxis(a.reshape((b, t // blk, blk) + a.shape[2:]), 1, 0)


def _unblocks(a):
    a = jnp.moveaxis(a, 0, 1)
    return a.reshape((a.shape[0], -1) + a.shape[3:])


def _sweep(fn, blk, arrays, q_pos):
    t = q_pos.shape[0]
    blk = min(blk, t)
    xs = tuple(_blocks(a, blk) for a in arrays) + (q_pos.reshape(t // blk, blk),)
    return _unblocks(lax.map(fn, xs))


def rwkv7_group(p, shift0, s0, mu, w0, w_up, a0, a_up, g_up, k_k, k_a, r_k, gn_w, gn_b):
    b, t, _ = p.shape
    f32 = jnp.float32
    prev = jnp.concatenate([shift0[:, None, :].astype(p.dtype), p[:, :-1]], axis=1)
    xm = p + (prev - p) * mu
    gw = GROUP_WIDTH
    cuts = [gw, 2 * gw, 3 * gw, 3 * gw + RWKV_DECAY_RANK, 3 * gw + RWKV_DECAY_RANK + RWKV_A_RANK]
    r, k, v, xw, xa, xg = jnp.split(xm, cuts, axis=-1)
    w_log = -jax.nn.softplus(-(w0 + jnp.tanh(xw) @ w_up)) - RWKV_DECAY_OFFSET
    decay = jnp.exp(-jnp.exp(w_log.astype(f32)))
    a = jax.nn.sigmoid(a0 + xa @ a_up)
    g = jax.nn.sigmoid(xg) @ g_up
    heads = lambda z: z.reshape(b, t, H_RWKV, HEAD_DIM).astype(f32)
    kk = heads(k * k_k)
    kk = kk * lax.rsqrt(jnp.maximum(jnp.sum(kk * kk, axis=-1, keepdims=True), 1e-24))
    k = k * (1.0 + (a - 1.0) * k_a)
    rh, wh, kh, vh, ah = heads(r), heads(decay), heads(k), heads(v), heads(a)

    def step(s, inp):
        r_t, w_t, k_t, v_t, kk_t, a_t = inp
        sa = jnp.einsum('bhij,bhj->bhi', s, -kk_t)
        s = (s * w_t[:, :, None, :] + sa[..., None] * (kk_t * a_t)[:, :, None, :]
             + v_t[..., None] * k_t[:, :, None, :])
        return s, jnp.einsum('bhij,bhj->bhi', s, r_t)

    xs = tuple(jnp.moveaxis(z, 1, 0) for z in (rh, wh, kh, vh, kk, ah))
    s_fin, y = lax.scan(step, s0.astype(f32), xs)
    y = jnp.moveaxis(y, 0, 1)
    mean = jnp.mean(y, axis=-1, keepdims=True)
    var = jnp.mean(jnp.square(y - mean), axis=-1, keepdims=True)
    y = ((y - mean) * lax.rsqrt(var + RWKV_GN_EPS)).reshape(b, t, gw) * gn_w + gn_b
    bonus = (jnp.sum(rh * kh * r_k, axis=-1, keepdims=True) * vh).reshape(b, t, gw)
    out = ((y + bonus) * g).astype(p.dtype)
    return out, p[:, -1], s_fin.astype(s0.dtype)


def fox_attention(q, k, v, c, blk):
    t, l = q.shape[1], k.shape[1]
    k_pos = jnp.arange(l)
    q_pos = l - t + jnp.arange(t)
    cq = c[:, l - t:]
    ck = jnp.moveaxis(c, 1, 2)
    scale = HEAD_DIM ** -0.5

    def one(args):
        qb, cqb, pb = args
        s = jnp.einsum('bqhd,bkhd->bhqk', qb, k).astype(jnp.float32) * scale
        s = s + jnp.moveaxis(cqb, 1, 2)[..., None] - ck[:, :, None, :]
        s = jnp.where(k_pos[None, :] <= pb[:, None], s, -jnp.inf)
        pr = jax.nn.softmax(s, axis=-1)
        return jnp.einsum('bhqk,bkhd->bqhd', pr.astype(v.dtype), v)

    return _sweep(one, blk, (q, cq), q_pos)


def stick_breaking_attention(q, k, v, blk):
    t, l = q.shape[1], k.shape[1]
    k_pos = jnp.arange(l)
    q_pos = l - t + jnp.arange(t)
    scale = HEAD_DIM ** -0.5

    def one(args):
        qb, pb = args
        z = jnp.einsum('bqhd,bkhd->bhqk', qb, k).astype(jnp.float32) * scale
        m = k_pos[None, :] < pb[:, None]
        log_keep = jnp.where(m, jax.nn.log_sigmoid(-z), 0.0)
        cum = jnp.cumsum(log_keep, axis=-1)
        log_a = jax.nn.log_sigmoid(z) + cum[..., -1:] - cum
        att = jnp.where(m, jnp.exp(log_a), 0.0)
        return jnp.einsum('bhqk,bkhd->bqhd', att.astype(v.dtype), v)

    return _sweep(one, blk, (q,), q_pos)


def moba_attention(q, k, v, blk):
    b, t, h, d = q.shape
    l = k.shape[1]
    n_blocks = max(-(-l // MOBA_BLOCK), MOBA_TOPK)
    pad = n_blocks * MOBA_BLOCK - l
    kp = jnp.pad(k, ((0, 0), (0, pad), (0, 0), (0, 0)))
    vp = jnp.pad(v, ((0, 0), (0, pad), (0, 0), (0, 0)))
    kb = jnp.transpose(kp.reshape(b, n_blocks, MOBA_BLOCK, h, d), (0, 3, 1, 2, 4))
    vb = jnp.transpose(vp.reshape(b, n_blocks, MOBA_BLOCK, h, d), (0, 3, 1, 2, 4))
    k_mean = jnp.mean(kb.astype(jnp.float32), axis=3)
    q_pos = l - t + jnp.arange(t)
    b_idx = jnp.arange(b)[:, None, None, None]
    h_idx = jnp.arange(h)[None, :, None, None]
    blk_range = jnp.arange(n_blocks)
    in_block = jnp.arange(MOBA_BLOCK)
    scale = HEAD_DIM ** -0.5

    def one(args):
        qb, pb = args
        nq = pb.shape[0]
        own = pb // MOBA_BLOCK
        gate = jnp.einsum('bqhd,bhnd->bhqn', qb.astype(jnp.float32), k_mean)
        gate = jnp.where(blk_range[None, :] < own[:, None], gate, -jnp.inf)
        _, top = lax.top_k(gate, MOBA_TOPK)
        top_ok = top < own[:, None]
        own_b = jnp.broadcast_to(own[:, None], (b, h, nq, 1))
        sel = jnp.concatenate([top, own_b], axis=-1)
        ok = jnp.concatenate([top_ok, jnp.ones(own_b.shape, bool)], axis=-1)
        kg = kb[b_idx, h_idx, sel]
        vg = vb[b_idx, h_idx, sel]
        key_pos = sel[..., None] * MOBA_BLOCK + in_block
        mask = ok[..., None] & (key_pos <= pb[:, None, None])
        s = jnp.einsum('bqhd,bhqnkd->bhqnk', qb, kg).astype(jnp.float32) * scale
        s = jnp.where(mask, s, -jnp.inf)
        pr = jax.nn.softmax(s, axis=(-2, -1))
        return jnp.einsum('bhqnk,bhqnkd->bqhd', pr.astype(vg.dtype), vg)

    return _sweep(one, blk, (q,), q_pos)


def decoder_layer(x, shift0, s0, kv_past, logf_past, lw):
    b, t, _ = x.shape
    a_w = H_ATT * HEAD_DIM
    n_qk = H_FOX + H_MOBA
    h = rms_norm(x, lw['attn_norm'])
    proj = h @ lw['w_in']
    p_rwkv = proj[..., :N_RWKV_COLS]
    q, k, v = (proj[..., N_RWKV_COLS + i * a_w:N_RWKV_COLS + (i + 1) * a_w].reshape(b, t, H_ATT, HEAD_DIM)
               for i in range(3))
    f_logit = proj[..., N_RWKV_COLS + 3 * a_w:]
    q = jnp.concatenate([rms_norm(q[:, :, :n_qk], lw['q_norm']), q[:, :, n_qk:]], axis=2)
    k = jnp.concatenate([rms_norm(k[:, :, :n_qk], lw['k_norm']), k[:, :, n_qk:]], axis=2)
    log_f = jax.nn.log_sigmoid((f_logit + lw['b_forget']).astype(jnp.float32))
    kv_new = jnp.stack([k, v], axis=2)
    if kv_past is None:
        kv_all, logf_all = kv_new, log_f
    else:
        kv_all = jnp.concatenate([kv_past.astype(kv_new.dtype), kv_new], axis=1)
        logf_all = jnp.concatenate([logf_past.astype(jnp.float32), log_f], axis=1)
    c_all = jnp.cumsum(logf_all, axis=1)
    k_all, v_all = kv_all[:, :, 0], kv_all[:, :, 1]

    y_rwkv, shift_new, s_new = rwkv7_group(
        p_rwkv, shift0, s0, lw['rwkv_mu'], lw['rwkv_w0'], lw['rwkv_w_up'], lw['rwkv_a0'],
        lw['rwkv_a_up'], lw['rwkv_g_up'], lw['rwkv_k_k'], lw['rwkv_k_a'], lw['rwkv_r_k'],
        lw['rwkv_gn_w'], lw['rwkv_gn_b'])
    fs, ms, ss = slice(0, H_FOX), slice(H_FOX, n_qk), slice(n_qk, H_ATT)
    y_fox = fox_attention(q[:, :, fs], k_all[:, :, fs], v_all[:, :, fs], c_all, Q_BLOCK)
    y_moba = moba_attention(q[:, :, ms], k_all[:, :, ms], v_all[:, :, ms], MOBA_Q_BLOCK)
    y_sb = stick_breaking_attention(q[:, :, ss], k_all[:, :, ss], v_all[:, :, ss], Q_BLOCK)
    y_att = jnp.concatenate([y_fox, y_moba, y_sb], axis=2).reshape(b, t, H_ATT * HEAD_DIM)
    mix = jnp.concatenate([y_rwkv, y_att.astype(y_rwkv.dtype)], axis=-1)
    x = x + mix @ lw['w_out']
    h2 = rms_norm(x, lw['mlp_norm'])
    x = x + jnp.square(jax.nn.relu(h2 @ lw['w_mlp_up'])) @ lw['w_mlp_down']
    return x, kv_new, log_f.astype(x.dtype), s_new, shift_new


def setup_inputs(seed: int = 0) -> dict:
    key = jax.random.key(seed)
    ks = jax.random.split(key, 32)
    f32 = jnp.float32

    def nrm(k, shape, scale=1.0):
        return scale * jax.random.normal(k, shape, f32)

    n_pages = PAST_LEN // PAGE_SIZE
    n_pool = (POOL_NUM * DEC_BATCH * n_pages + POOL_DEN - 1) // POOL_DEN
    L, GW = DEPTH, GROUP_WIDTH
    page_table = jax.random.permutation(ks[6], n_pool)[:DEC_BATCH * n_pages]
    page_table = page_table.reshape(DEC_BATCH, n_pages).astype(jnp.int32)
    return {
        'x_prompt': nrm(ks[0], (BATCH, SEQ, D_MODEL)),
        'x_sample': nrm(ks[1], (DEC_BATCH, DEC_SEQ, D_MODEL)),
        'cache_kv': nrm(ks[2], (DEPTH, n_pool, PAGE_SIZE, 2, H_ATT, HEAD_DIM)),
        'cache_logf': jax.nn.log_sigmoid(FORGET_BIAS + nrm(ks[3], (DEPTH, n_pool, PAGE_SIZE, H_FOX))),
        'state_wkv': nrm(ks[4], (DEPTH, DEC_BATCH, H_RWKV, HEAD_DIM, HEAD_DIM), 0.3),
        'state_shift': nrm(ks[5], (DEPTH, DEC_BATCH, N_RWKV_COLS)),
        'page_table': page_table,
        'attn_norm': 1.0 + nrm(ks[7], (L, D_MODEL), 0.05),
        'w_in': nrm(ks[8], (L, D_MODEL, N_IN), D_MODEL ** -0.5),
        'rwkv_mu': jax.random.uniform(ks[9], (L, N_RWKV_COLS), f32),
        'rwkv_w0': nrm(ks[10], (L, GW), 0.5) - 0.5,
        'rwkv_w_up': nrm(ks[11], (L, RWKV_DECAY_RANK, GW), RWKV_DECAY_RANK ** -0.5),
        'rwkv_a0': nrm(ks[12], (L, GW), 0.1),
        'rwkv_a_up': nrm(ks[13], (L, RWKV_A_RANK, GW), RWKV_A_RANK ** -0.5),
        'rwkv_g_up': nrm(ks[14], (L, RWKV_GATE_RANK, GW), RWKV_GATE_RANK ** -0.5),
        'rwkv_k_k': 0.85 + nrm(ks[15], (L, GW), 0.05),
        'rwkv_k_a': 1.0 + nrm(ks[16], (L, GW), 0.05),
        'rwkv_r_k': nrm(ks[17], (L, H_RWKV, HEAD_DIM), 0.1),
        'rwkv_gn_w': 1.0 + nrm(ks[18], (L, GW), 0.05),
        'rwkv_gn_b': nrm(ks[19], (L, GW), 0.02),
        'q_norm': 1.0 + nrm(ks[20], (L, H_FOX + H_MOBA, HEAD_DIM), 0.05),
        'k_norm': 1.0 + nrm(ks[21], (L, H_FOX + H_MOBA, HEAD_DIM), 0.05),
        'b_forget': FORGET_BIAS + nrm(ks[22], (L, H_FOX), 0.1),
        'w_out': nrm(ks[23], (L, MIX_WIDTH, D_MODEL), MIX_WIDTH ** -0.5),
        'mlp_norm': 1.0 + nrm(ks[24], (L, D_MODEL), 0.05),
        'w_mlp_up': nrm(ks[25], (L, D_MODEL, D_FF), D_MODEL ** -0.5),
        'w_mlp_down': nrm(ks[26], (L, D_FF, D_MODEL), D_FF ** -0.5),
    }


def reference(x_prompt, x_sample, cache_kv, cache_logf, state_wkv, state_shift, page_table,
              attn_norm, w_in, rwkv_mu, rwkv_w0, rwkv_w_up, rwkv_a0, rwkv_a_up, rwkv_g_up,
              rwkv_k_k, rwkv_k_a, rwkv_r_k, rwkv_gn_w, rwkv_gn_b, q_norm, k_norm, b_forget,
              w_out, mlp_norm, w_mlp_up, w_mlp_down):
    yp, ys = x_prompt, x_sample
    bp, db = x_prompt.shape[0], x_sample.shape[0]
    kvp_l, kvs_l, lfp_l, lfs_l, sp_l, ss_l, shp_l, shs_l = [], [], [], [], [], [], [], []
    for l in range(DEPTH):
        lw = dict(attn_norm=attn_norm[l], w_in=w_in[l], rwkv_mu=rwkv_mu[l], rwkv_w0=rwkv_w0[l],
                  rwkv_w_up=rwkv_w_up[l], rwkv_a0=rwkv_a0[l], rwkv_a_up=rwkv_a_up[l],
                  rwkv_g_up=rwkv_g_up[l], rwkv_k_k=rwkv_k_k[l], rwkv_k_a=rwkv_k_a[l],
                  rwkv_r_k=rwkv_r_k[l], rwkv_gn_w=rwkv_gn_w[l], rwkv_gn_b=rwkv_gn_b[l],
                  q_norm=q_norm[l], k_norm=k_norm[l], b_forget=b_forget[l], w_out=w_out[l],
                  mlp_norm=mlp_norm[l], w_mlp_up=w_mlp_up[l], w_mlp_down=w_mlp_down[l])
        shift0 = jnp.zeros((bp, N_RWKV_COLS), yp.dtype)
        s0 = jnp.zeros((bp, H_RWKV, HEAD_DIM, HEAD_DIM), yp.dtype)
        yp, kvp, lfp, sp, shp = decoder_layer(yp, shift0, s0, None, None, lw)
        past_kv = cache_kv[l, page_table].reshape(db, -1, 2, H_ATT, HEAD_DIM)
        past_lf = cache_logf[l, page_table].reshape(db, -1, H_FOX)
        ys, kvs, lfs, ss, shs = decoder_layer(ys, state_shift[l], state_wkv[l], past_kv, past_lf, lw)
        kvp_l.append(kvp); kvs_l.append(kvs); lfp_l.append(lfp); lfs_l.append(lfs)
        sp_l.append(sp); ss_l.append(ss); shp_l.append(shp); shs_l.append(shs)
    kv_prompt, kv_sample = jnp.stack(kvp_l), jnp.stack(kvs_l)
    logf_prompt, logf_sample = jnp.stack(lfp_l), jnp.stack(lfs_l)
    wkv_prompt, wkv_sample = jnp.stack(sp_l), jnp.stack(ss_l)
    shift_prompt, shift_sample = jnp.stack(shp_l), jnp.stack(shs_l)
    return (yp, ys, kv_prompt, kv_sample, logf_prompt, logf_sample, wkv_prompt, wkv_sample, shift_prompt, shift_sample)
```

```python
import functools

import jax
import jax.numpy as jnp
from jax import lax
from jax.experimental import pallas as pl
from jax.experimental.pallas import tpu as pltpu

F32 = jnp.float32
BF16 = jnp.bfloat16

HEAD_DIM = 64
GROUP_WIDTH = 256
N_GROUP_HEADS = 4
ATT_WIDTH = 3 * GROUP_WIDTH
N_RWKV_COLS = 1024
MOBA_BLOCK = 256
MOBA_TOPK = 3
RMS_EPS = 1e-6
RWKV_GN_EPS = 64e-5
RWKV_DECAY_OFFSET = 0.5
QK_SCALE = HEAD_DIM ** -0.5
NEG_BIG = -1e30

LANES = 128
RWKV_GROUP = 128
RWKV_CHUNK = 16
VMEM_LIMIT = 56 * 1024 * 1024


def _bf(x):
    return x.astype(BF16)


def _mm(a, b):
    return jnp.dot(_bf(a), _bf(b), preferred_element_type=F32)


def _mm_nt(a, b):
    return lax.dot_general(_bf(a), _bf(b), (((1,), (1,)), ((), ())), preferred_element_type=F32)


def _mm_tn(a, b):
    return lax.dot_general(_bf(a), _bf(b), (((0,), (0,)), ((), ())), preferred_element_type=F32)


def _split3(x):
    hi = _bf(x)
    r1 = x - hi.astype(F32)
    mid = _bf(r1)
    lo = _bf(r1 - mid.astype(F32))
    return hi, mid, lo


def _mm_exact_rhs(a, b01):
    hi, mid, lo = _split3(a)
    return (jnp.dot(hi, b01, preferred_element_type=F32) + jnp.dot(mid, b01, preferred_element_type=F32)
            + jnp.dot(lo, b01, preferred_element_type=F32))


def _mm_exact_lhs(a01, b):
    hi, mid, lo = _split3(b)
    return (jnp.dot(a01, hi, preferred_element_type=F32) + jnp.dot(a01, mid, preferred_element_type=F32)
            + jnp.dot(a01, lo, preferred_element_type=F32))


def _mm3(a, b):
    ah = _bf(a)
    al = _bf(a - ah.astype(F32))
    bh = _bf(b)
    bl = _bf(b - bh.astype(F32))
    return (jnp.dot(ah, bh, preferred_element_type=F32) + jnp.dot(ah, bl, preferred_element_type=F32)
            + jnp.dot(al, bh, preferred_element_type=F32))


def _mm3_nt(a, b):
    ah = _bf(a)
    al = _bf(a - ah.astype(F32))
    bh = _bf(b)
    bl = _bf(b - bh.astype(F32))
    dn = (((1,), (1,)), ((), ()))
    return (lax.dot_general(ah, bh, dn, preferred_element_type=F32)
            + lax.dot_general(ah, bl, dn, preferred_element_type=F32)
            + lax.dot_general(al, bh, dn, preferred_element_type=F32))


def _softplus(x):
    return jnp.maximum(x, 0.0) + jnp.log(1.0 + jnp.exp(-jnp.abs(x)))


def _log_sigmoid(x):
    return -_softplus(-x)


def _sigmoid(x):
    return 1.0 / (1.0 + jnp.exp(-x))


def _seg_ones(n, seg):
    r = lax.broadcasted_iota(jnp.int32, (n, n), 0) // seg
    c = lax.broadcasted_iota(jnp.int32, (n, n), 1) // seg
    return jnp.where(r == c, 1.0, 0.0).astype(BF16)


def _head_sum(x, seg01):
    return _mm_exact_rhs(x, seg01)


def _rms(x, g):
    return x * lax.rsqrt(jnp.mean(x * x, axis=-1, keepdims=True) + RMS_EPS) * g


def _const_spec(shape):
    nd = len(shape)
    return pl.BlockSpec(shape, lambda *_: (0,) * nd)


def _params(sem):
    return pltpu.CompilerParams(dimension_semantics=sem, vmem_limit_bytes=VMEM_LIMIT)


def _proj_kernel(*refs, prompt, tm, seq_len):
    if prompt:
        (x_ref, g_ref, wr_ref, wq_ref, wk_ref, wv_ref, wfh_ref, wfl_ref, qn_ref, kn_ref, bf_ref,
         p_ref, q_ref, kv_ref, lf_ref, c_ref, km_ref, carry_ref) = refs
    else:
        (x_ref, g_ref, wr_ref, wq_ref, wk_ref, wv_ref, wfh_ref, wfl_ref, qn_ref, kn_ref, bf_ref,
         p_ref, q_ref, kv_ref, lf_ref, c_ref) = refs
    h = _rms(x_ref[...], g_ref[...])
    hb = _bf(h)
    hl = _bf(h - hb.astype(F32))
    p_ref[...] = jnp.dot(hb, wr_ref[...], preferred_element_type=F32)
    q = jnp.dot(hb, wq_ref[...], preferred_element_type=F32)
    k = jnp.dot(hb, wk_ref[...], preferred_element_type=F32)
    v = jnp.dot(hb, wv_ref[...], preferred_element_type=F32)
    nw = 2 * GROUP_WIDTH
    seg01 = _seg_ones(nw, HEAD_DIM)

    def head_norm(z, gn):
        ms = _head_sum(z * z, seg01) * (1.0 / HEAD_DIM)
        return z * lax.rsqrt(ms + RMS_EPS) * gn

    kn = head_norm(k[:, :nw], kn_ref[...])
    q_ref[:, :nw] = head_norm(q[:, :nw], qn_ref[...])
    q_ref[:, nw:] = q[:, nw:]
    kv_ref[:, :nw] = kn
    kv_ref[:, nw:ATT_WIDTH] = k[:, nw:]
    kv_ref[:, ATT_WIDTH:] = v
    f = (jnp.dot(hb, wfh_ref[...], preferred_element_type=F32) + jnp.dot(hb, wfl_ref[...], preferred_element_type=F32)
         + jnp.dot(hl, wfh_ref[...], preferred_element_type=F32))
    lf = _log_sigmoid(f + bf_ref[...])
    lf_ref[...] = lf[:, :N_GROUP_HEADS]
    r = lax.broadcasted_iota(jnp.int32, (tm, tm), 0)
    c = lax.broadcasted_iota(jnp.int32, (tm, tm), 1)
    if prompt:
        @pl.when(pl.program_id(0) % (seq_len // tm) == 0)
        def _():
            carry_ref[...] = jnp.zeros_like(carry_ref)

        tri01 = jnp.where(c <= r, 1.0, 0.0).astype(BF16)
        cs = _mm_exact_lhs(tri01, lf) + carry_ref[...]
        carry_ref[...] = cs[tm - 1:tm, :]
        km = kn[:, GROUP_WIDTH:nw].reshape(tm // MOBA_BLOCK, MOBA_BLOCK, GROUP_WIDTH)
        km_ref[0] = jnp.sum(km, axis=1) * (1.0 / MOBA_BLOCK)
    else:
        tri01 = jnp.where((c <= r) & (r // seq_len == c // seq_len), 1.0, 0.0).astype(BF16)
        cs = _mm_exact_lhs(tri01, lf)
    c_ref[...] = cs[:, :N_GROUP_HEADS]


def _proj(x, lw, *, prompt, seq_len):
    n, d = x.shape
    tm = 512 if prompt else n
    grid = n // tm
    row = lambda w: pl.BlockSpec((tm, w), lambda i: (i, 0))
    in_specs = [row(d), _const_spec((1, d)), _const_spec((d, N_RWKV_COLS)), _const_spec((d, ATT_WIDTH)),
                _const_spec((d, ATT_WIDTH)), _const_spec((d, ATT_WIDTH)), _const_spec((d, LANES)),
                _const_spec((d, LANES)), _const_spec((1, 2 * GROUP_WIDTH)), _const_spec((1, 2 * GROUP_WIDTH)),
                _const_spec((1, LANES))]
    out_shape = [jax.ShapeDtypeStruct((n, N_RWKV_COLS), F32), jax.ShapeDtypeStruct((n, ATT_WIDTH), F32),
                 jax.ShapeDtypeStruct((n, 2 * ATT_WIDTH), F32), jax.ShapeDtypeStruct((n, N_GROUP_HEADS), F32),
                 jax.ShapeDtypeStruct((n, N_GROUP_HEADS), F32)]
    out_specs = [row(N_RWKV_COLS), row(ATT_WIDTH), row(2 * ATT_WIDTH), row(N_GROUP_HEADS), row(N_GROUP_HEADS)]
    scratch = []
    if prompt:
        nb = tm // MOBA_BLOCK
        out_shape += [jax.ShapeDtypeStruct((grid, nb, GROUP_WIDTH), F32)]
        out_specs += [pl.BlockSpec((1, nb, GROUP_WIDTH), lambda i: (i, 0, 0))]
        scratch = [pltpu.VMEM((1, LANES), F32)]
    return pl.pallas_call(
        functools.partial(_proj_kernel, prompt=prompt, tm=tm, seq_len=seq_len),
        grid=(grid,), in_specs=in_specs, out_specs=out_specs, out_shape=out_shape, scratch_shapes=scratch,
        compiler_params=_params(("arbitrary",)), name="proj_prompt" if prompt else "proj_sample",
    )(x, lw["attn_norm"], lw["w_r"], lw["w_q"], lw["w_k"], lw["w_v"], lw["w_f_hi"], lw["w_f_lo"],
      lw["q_norm"], lw["k_norm"], lw["b_forget"])


MLP_FF_CHUNK = 1024


def _mlp_kernel(*refs):
    x_ref, y_refs = refs[0], refs[1:-7]
    wo_ref, g_ref, wu_ref, wd_ref, o_ref, acc_scr, h_scr = refs[-7:]
    c = pl.program_id(1)

    @pl.when(c == 0)
    def _():
        x = x_ref[...]
        off = 0
        for y_ref in y_refs:
            w = y_ref.shape[1]
            x = x + jnp.dot(_bf(y_ref[...]), wo_ref[off:off + w, :], preferred_element_type=F32)
            off += w
        acc_scr[...] = x
        h_scr[...] = _bf(_rms(x, g_ref[...]))

    u = jnp.dot(h_scr[...], wu_ref[...], preferred_element_type=F32)
    u = jnp.square(jnp.maximum(u, 0.0))
    acc_scr[...] += jnp.dot(_bf(u), wd_ref[...], preferred_element_type=F32)

    @pl.when(c == pl.num_programs(1) - 1)
    def _():
        o_ref[...] = acc_scr[...]


def _mlp(x, ys, lw):
    n, d = x.shape
    d_ff = lw["w_up"].shape[1]
    tm = min(512, n)
    fc = MLP_FF_CHUNK
    row = lambda w: pl.BlockSpec((tm, w), lambda i, c: (i, 0))
    return pl.pallas_call(
        _mlp_kernel, grid=(n // tm, d_ff // fc),
        in_specs=([row(d)] + [row(y.shape[1]) for y in ys]
                  + [_const_spec((d, d)), _const_spec((1, d)),
                     pl.BlockSpec((d, fc), lambda i, c: (0, c)), pl.BlockSpec((fc, d), lambda i, c: (c, 0))]),
        out_specs=row(d), out_shape=jax.ShapeDtypeStruct((n, d), F32),
        scratch_shapes=[pltpu.VMEM((tm, d), F32), pltpu.VMEM((tm, d), BF16)],
        compiler_params=_params(("arbitrary", "arbitrary")), name="mlp",
    )(x, *ys, lw["w_out"], lw["mlp_norm"], lw["w_up"], lw["w_down"])


def _rwkv_kernel(p_ref, sh0_ref, s0_ref, mu_ref, w0_ref, a0_ref, wwa_ref, gup_ref, kk_ref, ka_ref, rk_ref,
                 gnw_ref, gnb_ref, y_ref, sfin_ref, s_scr, carry_scr, *, t_valid, chunk):
    t = pl.program_id(1)
    gsz = p_ref.shape[1]
    gw = GROUP_WIDTH

    @pl.when(t == 0)
    def _():
        carry_scr[...] = sh0_ref[0]
        s_scr[...] = s0_ref[0]

    p = p_ref[0]
    rows = lax.broadcasted_iota(jnp.int32, (gsz, 1), 0)
    prev = jnp.where(rows == 0, carry_scr[...], pltpu.roll(p, 1, axis=0))
    carry_scr[...] = p[gsz - 1:gsz, :]
    xm = p + (prev - p) * mu_ref[...]
    r, k, v = xm[:, :gw], xm[:, gw:2 * gw], xm[:, 2 * gw:3 * gw]
    xwa = xm[:, 3 * gw:3 * gw + LANES]
    xg = xm[:, 3 * gw + LANES:]
    lane_l = lax.broadcasted_iota(jnp.int32, (1, LANES), 1)
    lora = _mm3(jnp.where(lane_l < LANES // 2, jnp.tanh(xwa), xwa), wwa_ref[...])
    w_log = -_softplus(-(w0_ref[...] + lora[:, :gw])) - RWKV_DECAY_OFFSET
    logw = -jnp.exp(w_log)
    a = _sigmoid(a0_ref[...] + lora[:, gw:])
    g = _mm3(_sigmoid(xg), gup_ref[...])
    seg01 = _seg_ones(gw, HEAD_DIM)
    kk = k * kk_ref[...]
    kk = kk * lax.rsqrt(jnp.maximum(_head_sum(kk * kk, seg01), 1e-24))
    k2 = k * (1.0 + (a - 1.0) * ka_ref[...])
    bonus = _head_sum(r * k2 * rk_ref[...], seg01) * v
    if t_valid is not None:
        ok = (rows + t * gsz) < t_valid
        logw, kk, k2 = jnp.where(ok, logw, 0.0), jnp.where(ok, kk, 0.0), jnp.where(ok, k2, 0.0)
        r, v = jnp.where(ok, r, 0.0), jnp.where(ok, v, 0.0)
    b = kk * a

    ri = lax.broadcasted_iota(jnp.int32, (gsz, gsz), 0)
    ci = lax.broadcasted_iota(jnp.int32, (gsz, gsz), 1)
    same = (ri // chunk) == (ci // chunk)
    strict = same & (ci < ri)
    incl = same & (ci <= ri)
    lc = _mm_exact_lhs(jnp.where(incl, 1.0, 0.0).astype(BF16), logw)
    lcc = _mm_exact_lhs(jnp.where(same, 1.0, 0.0).astype(BF16), logw)
    e_neg = jnp.exp(-lc)
    e_end = jnp.exp(lcc - lc)
    at = -kk * jnp.exp(lc - logw)
    bt, kt = b * e_neg, k2 * e_neg
    rt = r * jnp.exp(lc)
    bd, kd = b * e_end, k2 * e_end
    g_end = jnp.exp(lcc)

    lane = lax.broadcasted_iota(jnp.int32, (1, gw), 1) // HEAD_DIM
    eye = jnp.where(ri == ci, 1.0, 0.0)
    rhs = jnp.concatenate([bt, kt], axis=0)
    wm = jnp.zeros((gsz, gw), F32)
    u0 = jnp.zeros((gsz, gw), F32)
    y0 = jnp.zeros((gsz, gw), F32)
    rbs = []
    n_sq = chunk.bit_length() - 2
    for h in range(N_GROUP_HEADS):
        mh = lane == h
        lhs = jnp.concatenate([jnp.where(mh, at, 0.0), jnp.where(mh, rt, 0.0)], axis=0)
        gram = _mm_nt(lhs, rhs)
        ab = jnp.where(strict, gram[:gsz, :gsz], 0.0)
        ak = jnp.where(strict, gram[:gsz, gsz:], 0.0)
        rbs.append(jnp.where(incl, gram[gsz:, :gsz], 0.0))
        rk = jnp.where(incl, gram[gsz:, gsz:], 0.0)
        tinv = eye + ab
        pw = ab
        for _ in range(n_sq):
            pw = _mm(pw, pw)
            tinv = tinv + _mm(tinv, pw)
        akv = _mm(jnp.concatenate([ak, rk], axis=0), v)
        tw = _mm(tinv, jnp.concatenate([at, akv[:gsz]], axis=1))
        wm = jnp.where(mh, tw[:, :gw], wm)
        u0 = jnp.where(mh, tw[:, gw:], u0)
        y0 = jnp.where(mh, akv[gsz:], y0)

    r2 = lax.broadcasted_iota(jnp.int32, (gw, gw), 0) // HEAD_DIM
    c2 = lax.broadcasted_iota(jnp.int32, (gw, gw), 1) // HEAD_DIM
    blockdiag = r2 == c2
    s = s_scr[...]
    us, zs = [], []
    for c in range(gsz // chunk):
        sl = slice(c * chunk, (c + 1) * chunk)
        z = _mm_nt(jnp.concatenate([wm[sl], rt[sl]], axis=0), s)
        u = z[:chunk] + u0[sl]
        us.append(u)
        zs.append(z[chunk:])
        upd = _mm_tn(jnp.concatenate([u, v[sl]], axis=0), jnp.concatenate([bd[sl], kd[sl]], axis=0))
        s = jnp.where(blockdiag, s * g_end[c * chunk:c * chunk + 1, :] + upd, 0.0)
    s_scr[...] = s
    u_all = jnp.concatenate(us, axis=0)
    y = jnp.concatenate(zs, axis=0) + y0
    for h in range(N_GROUP_HEADS):
        y = y + jnp.where(lane == h, _mm(rbs[h], u_all), 0.0)

    mean = _head_sum(y, seg01) * (1.0 / HEAD_DIM)
    yc = y - mean
    var = _head_sum(yc * yc, seg01) * (1.0 / HEAD_DIM)
    yn = yc * lax.rsqrt(var + RWKV_GN_EPS) * gnw_ref[...] + gnb_ref[...]
    y_ref[0] = (yn + bonus) * g

    @pl.when(t == pl.num_programs(1) - 1)
    def _():
        sfin_ref[0] = s


def _blockdiag_state(s):
    b = s.shape[0]
    eye = jnp.eye(N_GROUP_HEADS, dtype=s.dtype)
    return jnp.einsum("bhij,hg->bhigj", s, eye).reshape(b, GROUP_WIDTH, GROUP_WIDTH)


def _rwkv(p, shift0, s0, lw):
    bsz, t, _ = p.shape
    gsz = RWKV_GROUP
    t_pad = -(-t // gsz) * gsz
    if t_pad != t:
        p = jnp.pad(p, ((0, 0), (0, t_pad - t), (0, 0)))
    gw = GROUP_WIDTH
    vec = lambda w: _const_spec((1, w))
    y, s_bd = pl.pallas_call(
        functools.partial(_rwkv_kernel, t_valid=None if t_pad == t else t, chunk=RWKV_CHUNK),
        grid=(bsz, t_pad // gsz),
        in_specs=[pl.BlockSpec((1, gsz, N_RWKV_COLS), lambda b, i: (b, i, 0)),
                  pl.BlockSpec((1, 1, N_RWKV_COLS), lambda b, i: (b, 0, 0)),
                  pl.BlockSpec((1, gw, gw), lambda b, i: (b, 0, 0)),
                  vec(N_RWKV_COLS), vec(gw), vec(gw), _const_spec((LANES, 2 * gw)), _const_spec((LANES, gw)),
                  vec(gw), vec(gw), vec(gw), vec(gw), vec(gw)],
        out_specs=[pl.BlockSpec((1, gsz, gw), lambda b, i: (b, i, 0)),
                   pl.BlockSpec((1, gw, gw), lambda b, i: (b, 0, 0))],
        out_shape=[jax.ShapeDtypeStruct((bsz, t_pad, gw), F32), jax.ShapeDtypeStruct((bsz, gw, gw), F32)],
        scratch_shapes=[pltpu.VMEM((gw, gw), F32), pltpu.VMEM((1, N_RWKV_COLS), F32)],
        compiler_params=_params(("arbitrary", "arbitrary")), name="rwkv",
    )(p, shift0.reshape(bsz, 1, N_RWKV_COLS), _blockdiag_state(s0), lw["mu"], lw["w0"], lw["a0"], lw["w_wa"],
      lw["g_up"], lw["k_k"], lw["k_a"], lw["r_k"], lw["gn_w"], lw["gn_b"])
    s_fin = s_bd.reshape(bsz, N_GROUP_HEADS, HEAD_DIM, N_GROUP_HEADS, HEAD_DIM)
    s_fin = jnp.stack([s_fin[:, h, :, h, :] for h in range(N_GROUP_HEADS)], axis=1)
    return y[:, :t], s_fin


ATT_TILE = MOBA_BLOCK


def _head_lane(width=GROUP_WIDTH):
    return lax.broadcasted_iota(jnp.int32, (1, width), 1) // HEAD_DIM


def _flash_kernel(*refs, mode, n_blocks):
    if mode == "fox":
        q_ref, k_ref, v_ref, cq_ref, ck_ref, o_ref, m_scr, l_scr, acc_scr = refs
    else:
        q_ref, k_ref, v_ref, km_ref, o_ref, m_scr, l_scr, acc_scr, sel_scr = refs
    qi = pl.program_id(1)
    ki = pl.program_id(2)
    tq, tk = q_ref.shape[0], k_ref.shape[0]
    lane = _head_lane()

    @pl.when(ki == 0)
    def _():
        m_scr[...] = jnp.full(m_scr.shape, NEG_BIG, F32)
        l_scr[...] = jnp.zeros_like(l_scr)
        acc_scr[...] = jnp.zeros_like(acc_scr)
        if mode == "moba":
            q = q_ref[...]
            km = km_ref[0]
            col = lax.broadcasted_iota(jnp.int32, (1, LANES), 1)
            for h in range(N_GROUP_HEADS):
                gate = _mm3_nt(jnp.where(lane == h, q, 0.0), km)
                rank = jnp.zeros((tq, LANES), F32)
                for m in range(n_blocks):
                    gm = gate[:, m:m + 1]
                    beats = (gm > gate) | ((gm == gate) & (m < col))
                    rank = rank + jnp.where(beats, jnp.where(m < qi, 1.0, 0.0), 0.0)
                sel_scr[h] = jnp.where((col < qi) & (rank < MOBA_TOPK), 1.0, 0.0)

    @pl.when(ki <= qi)
    def _():
        q = q_ref[...] * QK_SCALE
        k = _bf(k_ref[...])
        v = _bf(v_ref[...])
        rowpos = lax.broadcasted_iota(jnp.int32, (tq, 1), 0) + qi * tq
        colpos = lax.broadcasted_iota(jnp.int32, (1, tk), 1) + ki * tk
        causal = colpos <= rowpos
        acc = acc_scr[...]
        for h in range(N_GROUP_HEADS):
            mh = lane == h
            s = _mm_nt(jnp.where(mh, q, 0.0), k)
            if mode == "fox":
                s = s + cq_ref[:, h:h + 1] - ck_ref[0, h:h + 1, :]
                mask = causal
            else:
                col = lax.broadcasted_iota(jnp.int32, (1, LANES), 1)
                picked = jnp.sum(jnp.where(col == ki, sel_scr[h], 0.0), axis=1, keepdims=True)
                picked = picked + jnp.where(ki == qi, 1.0, 0.0)
                mask = causal & (picked > 0.5)
            s = jnp.where(mask, s, NEG_BIG)
            m_old = m_scr[h]
            m_new = jnp.maximum(m_old, jnp.max(s, axis=1, keepdims=True))
            pr = jnp.exp(s - m_new)
            alpha = jnp.exp(m_old - m_new)
            l_scr[h] = alpha * l_scr[h] + jnp.sum(pr, axis=1, keepdims=True)
            m_scr[h] = m_new
            acc = jnp.where(mh, acc * alpha + _mm(pr, v), acc)
        acc_scr[...] = acc

    @pl.when(ki == pl.num_programs(2) - 1)
    def _():
        inv = jnp.zeros((tq, GROUP_WIDTH), F32)
        for h in range(N_GROUP_HEADS):
            inv = jnp.where(lane == h, 1.0 / l_scr[h], inv)
        o_ref[...] = acc_scr[...] * inv


def _flash(q, kv, *, mode, batch, seq_len, q_col, c=None, c_t=None, kmean=None):
    n = q.shape[0]
    t = ATT_TILE
    nq = seq_len // t
    gw = GROUP_WIDTH
    v_col = q_col + ATT_WIDTH // gw
    in_specs = [pl.BlockSpec((t, gw), lambda b, i, j: (b * nq + i, q_col)),
                pl.BlockSpec((t, gw), lambda b, i, j: (b * nq + jnp.minimum(i, j), q_col)),
                pl.BlockSpec((t, gw), lambda b, i, j: (b * nq + jnp.minimum(i, j), v_col))]
    scratch = [pltpu.VMEM((N_GROUP_HEADS, t, 1), F32), pltpu.VMEM((N_GROUP_HEADS, t, 1), F32),
               pltpu.VMEM((t, gw), F32)]
    if mode == "fox":
        in_specs += [pl.BlockSpec((t, N_GROUP_HEADS), lambda b, i, j: (b * nq + i, 0)),
                     pl.BlockSpec((1, N_GROUP_HEADS, t), lambda b, i, j: (b, 0, jnp.minimum(i, j)))]
        args = (q, kv, kv, c, c_t)
    else:
        in_specs += [pl.BlockSpec((1, LANES, gw), lambda b, i, j: (b, 0, 0))]
        scratch += [pltpu.VMEM((N_GROUP_HEADS, t, LANES), F32)]
        args = (q, kv, kv, kmean)
    return pl.pallas_call(
        functools.partial(_flash_kernel, mode=mode, n_blocks=nq),
        grid=(batch, nq, nq), in_specs=in_specs,
        out_specs=pl.BlockSpec((t, gw), lambda b, i, j: (b * nq + i, 0)),
        out_shape=jax.ShapeDtypeStruct((n, gw), F32), scratch_shapes=scratch,
        compiler_params=_params(("arbitrary", "arbitrary", "arbitrary")), name="attn_" + mode,
    )(*args)


def _sb_kernel(q_ref, k_ref, v_ref, o_ref, r_scr, acc_scr):
    qi = pl.program_id(1)
    step = pl.program_id(2)
    tq, tk = q_ref.shape[0], k_ref.shape[0]
    lane = _head_lane()

    @pl.when(step == 0)
    def _():
        r_scr[...] = jnp.zeros_like(r_scr)
        acc_scr[...] = jnp.zeros_like(acc_scr)

    @pl.when(step <= qi)
    def _():
        kb = qi - step
        q = q_ref[...] * QK_SCALE
        k = _bf(k_ref[...])
        v = _bf(v_ref[...])
        rowpos = lax.broadcasted_iota(jnp.int32, (tq, 1), 0) + qi * tq
        colpos = lax.broadcasted_iota(jnp.int32, (1, tk), 1) + kb * tk
        mask = colpos < rowpos
        later01 = jnp.where(lax.broadcasted_iota(jnp.int32, (tk, tk), 0) > lax.broadcasted_iota(jnp.int32, (tk, tk), 1),
                            1.0, 0.0).astype(BF16)
        acc = acc_scr[...]
        for h in range(N_GROUP_HEADS):
            mh = lane == h
            z = _mm_nt(jnp.where(mh, q, 0.0), k)
            log_keep = jnp.where(mask, -_softplus(z), 0.0)
            right = _mm_exact_rhs(log_keep, later01) + r_scr[h]
            att = jnp.where(mask, jnp.exp(-_softplus(-z) + right), 0.0)
            acc = jnp.where(mh, acc + _mm(att, v), acc)
            r_scr[h] = r_scr[h] + jnp.sum(log_keep, axis=1, keepdims=True)
        acc_scr[...] = acc

    @pl.when(step == pl.num_programs(2) - 1)
    def _():
        o_ref[...] = acc_scr[...]


def _sb(q, kv, *, batch, seq_len, q_col):
    n = q.shape[0]
    t = ATT_TILE
    nq = seq_len // t
    gw = GROUP_WIDTH
    v_col = q_col + ATT_WIDTH // gw
    kblk = lambda b, i, j: b * nq + jnp.maximum(i - j, 0)
    return pl.pallas_call(
        _sb_kernel, grid=(batch, nq, nq),
        in_specs=[pl.BlockSpec((t, gw), lambda b, i, j: (b * nq + i, q_col)),
                  pl.BlockSpec((t, gw), lambda b, i, j: (kblk(b, i, j), q_col)),
                  pl.BlockSpec((t, gw), lambda b, i, j: (kblk(b, i, j), v_col))],
        out_specs=pl.BlockSpec((t, gw), lambda b, i, j: (b * nq + i, 0)),
        out_shape=jax.ShapeDtypeStruct((n, gw), F32),
        scratch_shapes=[pltpu.VMEM((N_GROUP_HEADS, t, 1), F32), pltpu.VMEM((t, gw), F32)],
        compiler_params=_params(("arbitrary", "arbitrary", "arbitrary")), name="attn_sb",
    )(q, kv, kv)


PAGE = 128
LOGF_PAGES_PER_STEP = 16


def _foxpast_kernel(pt_ref, *refs):
    n_grp = LOGF_PAGES_PER_STEP
    page_refs, o_ref, x_scr, carry_scr = refs[:n_grp], refs[n_grp], refs[n_grp + 1], refs[n_grp + 2]
    del pt_ref

    @pl.when(pl.program_id(1) == 0)
    def _():
        carry_scr[...] = jnp.zeros_like(carry_scr)

    for i in range(n_grp):
        x_scr[i:i + 1, :] = page_refs[i][0]
    x = x_scr[...]
    nh = N_GROUP_HEADS
    li = lax.broadcasted_iota(jnp.int32, (PAGE * nh, PAGE), 0)
    ki = lax.broadcasted_iota(jnp.int32, (PAGE * nh, PAGE), 1)
    later01 = jnp.where(lax.broadcasted_iota(jnp.int32, (PAGE, PAGE), 0) > lax.broadcasted_iota(jnp.int32, (PAGE, PAGE), 1),
                        1.0, 0.0).astype(BF16)
    later_pg01 = jnp.where(lax.broadcasted_iota(jnp.int32, (n_grp, n_grp), 1) > lax.broadcasted_iota(jnp.int32, (n_grp, n_grp), 0),
                           1.0, 0.0).astype(BF16)
    for h in range(nh):
        pick01 = jnp.where((li // nh == ki) & (li % nh == h), 1.0, 0.0).astype(BF16)
        lh = _mm_exact_rhs(x, pick01)
        within = _mm_exact_rhs(lh, later01)
        later_pages = jnp.sum(_mm_exact_lhs(later_pg01, lh), axis=1, keepdims=True)
        total = jnp.sum(jnp.sum(lh, axis=1, keepdims=True), axis=0, keepdims=True)
        carry = carry_scr[h:h + 1, :]
        o_ref[0, h] = within + later_pages + carry
        carry_scr[h:h + 1, :] = carry + total


def _fox_past(page_table, logf_pages, page_offset):
    bsz, n_pages = page_table.shape
    n_grp = LOGF_PAGES_PER_STEP
    groups = n_pages // n_grp

    def page_spec(i):
        return pl.BlockSpec((1, 1, PAGE * N_GROUP_HEADS),
                            lambda b, g, pt: (page_offset + pt[b * n_pages + (groups - 1 - g) * n_grp + i], 0, 0))

    out = pl.pallas_call(
        _foxpast_kernel,
        grid_spec=pltpu.PrefetchScalarGridSpec(
            num_scalar_prefetch=1, grid=(bsz, groups),
            in_specs=[page_spec(i) for i in range(n_grp)],
            out_specs=pl.BlockSpec((1, N_GROUP_HEADS, n_grp, PAGE), lambda b, g, pt: (b, 0, groups - 1 - g, 0)),
            scratch_shapes=[pltpu.VMEM((n_grp, PAGE * N_GROUP_HEADS), F32), pltpu.VMEM((8, PAGE), F32)]),
        out_shape=jax.ShapeDtypeStruct((bsz, N_GROUP_HEADS, n_pages, PAGE), F32),
        compiler_params=_params(("arbitrary", "arbitrary")), name="fox_past",
    )(page_table.reshape(-1), *([logf_pages] * n_grp))
    return out.reshape(bsz, N_GROUP_HEADS, n_pages * PAGE)


N_PAIRS = ATT_WIDTH // LANES
PAIRS_PER_MIXER = 2


def _decode_kernel(pt_ref, q_ref, kvn_ref, cn_ref, cnt_ref, sfx_ref, pg0_ref, pg1_ref, o_ref,
                   fm_scr, fl_scr, facc_scr, sr_scr, sacc_scr, mg_scr, mm_scr, ml_scr, mo_scr,
                   om_scr, ol_scr, oo_scr, *, n_blocks):
    del pt_ref
    s_id = pl.program_id(1)
    nq = q_ref.shape[0]
    rows = 2 * nq
    lane = lax.broadcasted_iota(jnp.int32, (1, LANES), 1)
    low = lane < HEAD_DIM
    qrow = lax.broadcasted_iota(jnp.int32, (rows, 1), 0) % nq

    def pair_q(p):
        qp = q_ref[:, p * LANES:(p + 1) * LANES]
        return jnp.concatenate([jnp.where(low, qp, 0.0), jnp.where(low, 0.0, qp)], axis=0)

    def pair_col(x, p):
        return jnp.concatenate([x[:, 2 * p:2 * p + 1], x[:, 2 * p + 1:2 * p + 2]], axis=0)

    def pair_row(x, p, nk):
        return jnp.concatenate([jnp.broadcast_to(x[2 * p:2 * p + 1, :], (nq, nk)),
                                jnp.broadcast_to(x[2 * p + 1:2 * p + 2, :], (nq, nk))], axis=0)

    @pl.when(s_id == 0)
    def _():
        fm_scr[...] = jnp.full(fm_scr.shape, NEG_BIG, F32)
        fl_scr[...] = jnp.zeros_like(fl_scr)
        facc_scr[...] = jnp.zeros_like(facc_scr)
        sr_scr[...] = jnp.zeros_like(sr_scr)
        sacc_scr[...] = jnp.zeros_like(sacc_scr)
        mg_scr[...] = jnp.zeros_like(mg_scr)
        mm_scr[...] = jnp.zeros_like(mm_scr)
        ml_scr[...] = jnp.zeros_like(ml_scr)

    def attend(k_tile, v_tile, fox_rows, mask_incl, mask_strict, block):
        nk = k_tile.shape[0]
        kb = _bf(k_tile)
        vb = _bf(v_tile)
        later01 = jnp.where(lax.broadcasted_iota(jnp.int32, (nk, nk), 0) > lax.broadcasted_iota(jnp.int32, (nk, nk), 1),
                            1.0, 0.0).astype(BF16)
        for p in range(N_PAIRS):
            q2 = pair_q(p)
            kp = kb[:, p * LANES:(p + 1) * LANES]
            vp = vb[:, p * LANES:(p + 1) * LANES]
            z = _mm_nt(q2 * QK_SCALE, kp)
            mixer, j = divmod(p, PAIRS_PER_MIXER)
            if mixer == 0:
                sc = z + pair_col(cn_ref[...], j) + pair_row(fox_rows, j, nk)
                if mask_incl is not None:
                    sc = jnp.where(mask_incl, sc, NEG_BIG)
                m_old = fm_scr[j]
                m_new = jnp.maximum(m_old, jnp.max(sc, axis=1, keepdims=True))
                pr = jnp.exp(sc - m_new)
                alpha = jnp.exp(m_old - m_new)
                fl_scr[j] = alpha * fl_scr[j] + jnp.sum(pr, axis=1, keepdims=True)
                fm_scr[j] = m_new
                facc_scr[j] = alpha * facc_scr[j] + _mm(pr, vp)
            elif mixer == 1:
                sc = z if mask_incl is None else jnp.where(mask_incl, z, NEG_BIG)
                m_b = jnp.max(sc, axis=1, keepdims=True)
                pr = jnp.exp(sc - m_b)
                l_b = jnp.sum(pr, axis=1, keepdims=True)
                o_b = _mm(pr, vp)
                if block is None:
                    om_scr[j], ol_scr[j], oo_scr[j] = m_b, l_b, o_b
                else:
                    kmean = jnp.sum(k_tile[:, p * LANES:(p + 1) * LANES], axis=0, keepdims=True) * (1.0 / MOBA_BLOCK)
                    gate = jnp.sum(q2 * kmean, axis=1, keepdims=True)
                    here = lane == block
                    mg_scr[j] = jnp.where(here, gate, mg_scr[j])
                    mm_scr[j] = jnp.where(here, m_b, mm_scr[j])
                    ml_scr[j] = jnp.where(here, l_b, ml_scr[j])
                    mo_scr[j, pl.ds(block, 1)] = o_b[None]
            else:
                keep = -_softplus(z)
                if mask_strict is not None:
                    keep = jnp.where(mask_strict, keep, 0.0)
                right = _mm_exact_rhs(keep, later01) + sr_scr[j]
                att = jnp.exp(-_softplus(-z) + right)
                if mask_strict is not None:
                    att = jnp.where(mask_strict, att, 0.0)
                sacc_scr[j] = sacc_scr[j] + _mm(att, vp)
                sr_scr[j] = sr_scr[j] + jnp.sum(keep, axis=1, keepdims=True)

    @pl.when(s_id == 0)
    def _():
        pad = jnp.zeros((PAGE - nq, 2 * ATT_WIDTH), F32)
        kvn = jnp.concatenate([kvn_ref[...], pad], axis=0)
        key = lax.broadcasted_iota(jnp.int32, (1, PAGE), 1)
        attend(kvn[:, :ATT_WIDTH], kvn[:, ATT_WIDTH:], -cnt_ref[0], key <= qrow, key < qrow, None)

    block = n_blocks - 1 - s_id
    k_tile = jnp.concatenate([pg0_ref[0, :, :ATT_WIDTH], pg1_ref[0, :, :ATT_WIDTH]], axis=0)
    v_tile = jnp.concatenate([pg0_ref[0, :, ATT_WIDTH:], pg1_ref[0, :, ATT_WIDTH:]], axis=0)
    attend(k_tile, v_tile, sfx_ref[0], None, None, block)

    @pl.when(s_id == pl.num_programs(1) - 1)
    def _():
        def store(p, res):
            o_ref[:, p * LANES:(p + 1) * LANES] = jnp.where(low, res[:nq], res[nq:])

        for j in range(PAIRS_PER_MIXER):
            store(j, facc_scr[j] / fl_scr[j])
            store(2 * PAIRS_PER_MIXER + j, sacc_scr[j])
            gate, m_all, l_all = mg_scr[j], mm_scr[j], ml_scr[j]
            rank = jnp.zeros((rows, LANES), F32)
            for m in range(n_blocks):
                gm = gate[:, m:m + 1]
                rank = rank + jnp.where((gm > gate) | ((gm == gate) & (m < lane)), 1.0, 0.0)
            sel = (lane < n_blocks) & (rank < MOBA_TOPK)
            m_own = om_scr[j]
            m_top = jnp.maximum(jnp.max(jnp.where(sel, m_all, NEG_BIG), axis=1, keepdims=True), m_own)
            w = jnp.where(sel, jnp.exp(m_all - m_top), 0.0)
            w_own = jnp.exp(m_own - m_top)
            den = jnp.sum(w * l_all, axis=1, keepdims=True) + w_own * ol_scr[j]
            num = w_own * oo_scr[j]
            for n in range(n_blocks):
                num = num + w[:, n:n + 1] * mo_scr[j, n]
            store(PAIRS_PER_MIXER + j, num / den)


def _decode(page_table, q, kv_new, c_new, sfx, kv_pages, page_offset):
    bsz, n_pages = page_table.shape
    nq = q.shape[0] // bsz
    n_blocks = n_pages * PAGE // MOBA_BLOCK
    pages_per_block = MOBA_BLOCK // PAGE
    assert pages_per_block == 2 and n_blocks <= LANES
    cnt = jnp.pad(jnp.swapaxes(c_new.reshape(bsz, nq, N_GROUP_HEADS), 1, 2), ((0, 0), (0, 0), (0, PAGE - nq)))

    def page_spec(r):
        return pl.BlockSpec((1, PAGE, 2 * ATT_WIDTH),
                            lambda b, s, pt: (page_offset + pt[b * n_pages + 2 * (n_blocks - 1 - s) + r], 0, 0))

    rows = 2 * nq
    pm = PAIRS_PER_MIXER
    col = lambda: pltpu.VMEM((pm, rows, 1), F32)
    wide = lambda: pltpu.VMEM((pm, rows, LANES), F32)
    return pl.pallas_call(
        functools.partial(_decode_kernel, n_blocks=n_blocks),
        grid_spec=pltpu.PrefetchScalarGridSpec(
            num_scalar_prefetch=1, grid=(bsz, n_blocks),
            in_specs=[pl.BlockSpec((nq, ATT_WIDTH), lambda b, s, pt: (b, 0)),
                      pl.BlockSpec((nq, 2 * ATT_WIDTH), lambda b, s, pt: (b, 0)),
                      pl.BlockSpec((nq, N_GROUP_HEADS), lambda b, s, pt: (b, 0)),
                      pl.BlockSpec((1, N_GROUP_HEADS, PAGE), lambda b, s, pt: (b, 0, 0)),
                      pl.BlockSpec((1, N_GROUP_HEADS, MOBA_BLOCK), lambda b, s, pt: (b, 0, n_blocks - 1 - s)),
                      page_spec(0), page_spec(1)],
            out_specs=pl.BlockSpec((nq, ATT_WIDTH), lambda b, s, pt: (b, 0)),
            scratch_shapes=[col(), col(), wide(), col(), wide(), wide(), wide(), wide(),
                            pltpu.VMEM((pm, n_blocks, rows, LANES), F32), col(), col(), wide()]),
        out_shape=jax.ShapeDtypeStruct((bsz * nq, ATT_WIDTH), F32),
        compiler_params=_params(("arbitrary", "arbitrary")), name="decode_attn",
    )(page_table.reshape(-1), q, kv_new, c_new, cnt, sfx, kv_pages, kv_pages)


def _layer_weights(l, attn_norm, w_in, rwkv_mu, rwkv_w0, rwkv_w_up, rwkv_a0, rwkv_a_up, rwkv_g_up,
                   rwkv_k_k, rwkv_k_a, rwkv_r_k, rwkv_gn_w, rwkv_gn_b, q_norm, k_norm, b_forget,
                   w_out, mlp_norm, w_mlp_up, w_mlp_down):
    w = w_in[l]
    a0, a1, a2, a3 = N_RWKV_COLS, N_RWKV_COLS + ATT_WIDTH, N_RWKV_COLS + 2 * ATT_WIDTH, N_RWKV_COLS + 3 * ATT_WIDTH
    w_f = jnp.pad(w[:, a3:], ((0, 0), (0, LANES - N_GROUP_HEADS)))
    w_f_hi = w_f.astype(BF16)
    rank = rwkv_w_up.shape[1]
    z = jnp.zeros((rank, GROUP_WIDTH), F32)
    w_wa = jnp.concatenate([jnp.concatenate([rwkv_w_up[l], z], axis=1),
                            jnp.concatenate([z, rwkv_a_up[l]], axis=1)], axis=0)
    row = lambda v: v.reshape(1, -1)
    return dict(
        attn_norm=row(attn_norm[l]), w_r=w[:, :a0].astype(BF16), w_q=w[:, a0:a1].astype(BF16),
        w_k=w[:, a1:a2].astype(BF16), w_v=w[:, a2:a3].astype(BF16), w_f_hi=w_f_hi,
        w_f_lo=(w_f - w_f_hi.astype(F32)).astype(BF16), q_norm=row(q_norm[l]), k_norm=row(k_norm[l]),
        b_forget=jnp.pad(row(b_forget[l]), ((0, 0), (0, LANES - N_GROUP_HEADS))),
        mu=row(rwkv_mu[l]), w0=row(rwkv_w0[l]), a0=row(rwkv_a0[l]), w_wa=w_wa, g_up=rwkv_g_up[l],
        k_k=row(rwkv_k_k[l]), k_a=row(rwkv_k_a[l]), r_k=row(rwkv_r_k[l]), gn_w=row(rwkv_gn_w[l]),
        gn_b=row(rwkv_gn_b[l]), w_out=w_out[l].astype(BF16), mlp_norm=row(mlp_norm[l]),
        w_up=w_mlp_up[l].astype(BF16), w_down=w_mlp_down[l].astype(BF16))


def _decoder_layer(x, shift0, s0, lw, past):
    bsz, t, d = x.shape
    n = bsz * t
    xf = x.reshape(n, d)
    prompt = past is None
    outs = _proj(xf, lw, prompt=prompt, seq_len=t)
    p, q, kv, lf, c = outs[:5]
    p3 = p.reshape(bsz, t, N_RWKV_COLS)
    y_rwkv, s_new = _rwkv(p3, shift0, s0, lw)
    if prompt:
        kmean = outs[5].reshape(bsz, t // MOBA_BLOCK, GROUP_WIDTH)
        kmean = jnp.pad(kmean, ((0, 0), (0, LANES - t // MOBA_BLOCK), (0, 0)))
        c_t = jnp.swapaxes(c.reshape(bsz, t, N_GROUP_HEADS), 1, 2)
        ys = [_flash(q, kv, mode="fox", batch=bsz, seq_len=t, q_col=0, c=c, c_t=c_t),
              _flash(q, kv, mode="moba", batch=bsz, seq_len=t, q_col=1, kmean=kmean),
              _sb(q, kv, batch=bsz, seq_len=t, q_col=2)]
    else:
        page_table, kv_pages, logf_pages, page_offset = past
        sfx = _fox_past(page_table, logf_pages, page_offset)
        ys = [_decode(page_table, q, kv, c, sfx, kv_pages, page_offset)]
    x_new = _mlp(xf, [y_rwkv.reshape(n, GROUP_WIDTH)] + ys, lw)
    kv_new = kv.reshape(bsz, t, 2, ATT_WIDTH // HEAD_DIM, HEAD_DIM)
    return x_new.reshape(bsz, t, d), kv_new, lf.reshape(bsz, t, N_GROUP_HEADS), s_new, p3[:, -1]


def kernel(x_prompt, x_sample, cache_kv, cache_logf, state_wkv, state_shift, page_table, attn_norm, w_in, rwkv_mu,
           rwkv_w0, rwkv_w_up, rwkv_a0, rwkv_a_up, rwkv_g_up, rwkv_k_k, rwkv_k_a, rwkv_r_k, rwkv_gn_w, rwkv_gn_b,
           q_norm, k_norm, b_forget, w_out, mlp_norm, w_mlp_up, w_mlp_down):
    depth, n_pool = cache_kv.shape[:2]
    bp = x_prompt.shape[0]
    kv_pages = cache_kv.reshape(depth * n_pool, PAGE, 2 * ATT_WIDTH)
    logf_pages = cache_logf.reshape(depth * n_pool, 1, PAGE * N_GROUP_HEADS)
    yp, ys = x_prompt, x_sample
    outs = [[] for _ in range(8)]
    for l in range(depth):
        lw = _layer_weights(l, attn_norm, w_in, rwkv_mu, rwkv_w0, rwkv_w_up, rwkv_a0, rwkv_a_up, rwkv_g_up,
                            rwkv_k_k, rwkv_k_a, rwkv_r_k, rwkv_gn_w, rwkv_gn_b, q_norm, k_norm, b_forget,
                            w_out, mlp_norm, w_mlp_up, w_mlp_down)
        shift0 = jnp.zeros((bp, N_RWKV_COLS), yp.dtype)
        s0 = jnp.zeros((bp, N_GROUP_HEADS, HEAD_DIM, HEAD_DIM), yp.dtype)
        yp, kvp, lfp, sp, shp = _decoder_layer(yp, shift0, s0, lw, None)
        past = (page_table, kv_pages, logf_pages, l * n_pool)
        ys, kvs, lfs, ss, shs = _decoder_layer(ys, state_shift[l], state_wkv[l], lw, past)
        for acc, val in zip(outs, (kvp, kvs, lfp, lfs, sp, ss, shp, shs)):
            acc.append(val)
    return (yp, ys) + tuple(jnp.stack(o) for o in outs)
```

```python
import functools

import jax
import jax.numpy as jnp
from jax import lax
from jax.experimental import pallas as pl
from jax.experimental.pallas import tpu as pltpu

F32 = jnp.float32
BF16 = jnp.bfloat16

HEAD_DIM = 64
GROUP_WIDTH = 256
N_GROUP_HEADS = 4
ATT_WIDTH = 3 * GROUP_WIDTH
N_RWKV_COLS = 1024
MOBA_BLOCK = 256
MOBA_TOPK = 3
RMS_EPS = 1e-6
RWKV_GN_EPS = 64e-5
RWKV_DECAY_OFFSET = 0.5
QK_SCALE = HEAD_DIM ** -0.5
NEG_BIG = -1e30

LANES = 128
RWKV_GROUP = 128
RWKV_CHUNK = 16
RWKV_SEQS_PER_STEP = 2
VMEM_LIMIT = 56 * 1024 * 1024


def _bf(x):
    return x.astype(BF16)


def _mm(a, b):
    return jnp.dot(_bf(a), _bf(b), preferred_element_type=F32)


def _mm_nt(a, b):
    return lax.dot_general(_bf(a), _bf(b), (((1,), (1,)), ((), ())), preferred_element_type=F32)


def _mm_tn(a, b):
    return lax.dot_general(_bf(a), _bf(b), (((0,), (0,)), ((), ())), preferred_element_type=F32)


def _split3(x):
    hi = _bf(x)
    r1 = x - hi.astype(F32)
    mid = _bf(r1)
    lo = _bf(r1 - mid.astype(F32))
    return hi, mid, lo


def _mm_exact_rhs(a, b01):
    hi, mid, lo = _split3(a)
    return (jnp.dot(hi, b01, preferred_element_type=F32) + jnp.dot(mid, b01, preferred_element_type=F32)
            + jnp.dot(lo, b01, preferred_element_type=F32))


def _mm_exact_lhs(a01, b):
    hi, mid, lo = _split3(b)
    return (jnp.dot(a01, hi, preferred_element_type=F32) + jnp.dot(a01, mid, preferred_element_type=F32)
            + jnp.dot(a01, lo, preferred_element_type=F32))


def _mm3(a, b):
    ah = _bf(a)
    al = _bf(a - ah.astype(F32))
    bh = _bf(b)
    bl = _bf(b - bh.astype(F32))
    return (jnp.dot(ah, bh, preferred_element_type=F32) + jnp.dot(ah, bl, preferred_element_type=F32)
            + jnp.dot(al, bh, preferred_element_type=F32))


def _mm3_nt(a, b):
    ah = _bf(a)
    al = _bf(a - ah.astype(F32))
    bh = _bf(b)
    bl = _bf(b - bh.astype(F32))
    dn = (((1,), (1,)), ((), ()))
    return (lax.dot_general(ah, bh, dn, preferred_element_type=F32)
            + lax.dot_general(ah, bl, dn, preferred_element_type=F32)
            + lax.dot_general(al, bh, dn, preferred_element_type=F32))


def _softplus(x):
    return jnp.maximum(x, 0.0) + jnp.log(1.0 + jnp.exp(-jnp.abs(x)))


def _log_sigmoid(x):
    return -_softplus(-x)


def _sigmoid(x):
    return 1.0 / (1.0 + jnp.exp(-x))


def _seg_ones(n, seg):
    r = lax.broadcasted_iota(jnp.int32, (n, n), 0) // seg
    c = lax.broadcasted_iota(jnp.int32, (n, n), 1) // seg
    return jnp.where(r == c, 1.0, 0.0).astype(BF16)


def _head_sum(x, seg01):
    return _mm_exact_rhs(x, seg01)


def _rms(x, g):
    return x * lax.rsqrt(jnp.mean(x * x, axis=-1, keepdims=True) + RMS_EPS) * g


def _const_spec(shape):
    nd = len(shape)
    return pl.BlockSpec(shape, lambda *_: (0,) * nd)


def _params(sem):
    return pltpu.CompilerParams(dimension_semantics=sem, vmem_limit_bytes=VMEM_LIMIT)


def _proj_kernel(*refs, prompt, tm, seq_len):
    if prompt:
        (x_ref, g_ref, wr_ref, wq_ref, wk_ref, wv_ref, wfh_ref, wfl_ref, qn_ref, kn_ref, bf_ref,
         p_ref, q_ref, kv_ref, lf_ref, c_ref, km_ref, carry_ref) = refs
    else:
        (x_ref, g_ref, wr_ref, wq_ref, wk_ref, wv_ref, wfh_ref, wfl_ref, qn_ref, kn_ref, bf_ref,
         p_ref, q_ref, kv_ref, lf_ref, c_ref) = refs
    h = _rms(x_ref[...], g_ref[...])
    hb = _bf(h)
    hl = _bf(h - hb.astype(F32))
    p_ref[...] = jnp.dot(hb, wr_ref[...], preferred_element_type=F32)
    q = jnp.dot(hb, wq_ref[...], preferred_element_type=F32)
    k = jnp.dot(hb, wk_ref[...], preferred_element_type=F32)
    v = jnp.dot(hb, wv_ref[...], preferred_element_type=F32)
    nw = 2 * GROUP_WIDTH
    seg01 = _seg_ones(nw, HEAD_DIM)

    def head_norm(z, gn):
        ms = _head_sum(z * z, seg01) * (1.0 / HEAD_DIM)
        return z * lax.rsqrt(ms + RMS_EPS) * gn

    kn = head_norm(k[:, :nw], kn_ref[...])
    q_ref[:, :nw] = head_norm(q[:, :nw], qn_ref[...])
    q_ref[:, nw:] = q[:, nw:]
    kv_ref[:, :nw] = kn
    kv_ref[:, nw:ATT_WIDTH] = k[:, nw:]
    kv_ref[:, ATT_WIDTH:] = v
    f = (jnp.dot(hb, wfh_ref[...], preferred_element_type=F32) + jnp.dot(hb, wfl_ref[...], preferred_element_type=F32)
         + jnp.dot(hl, wfh_ref[...], preferred_element_type=F32))
    lf = _log_sigmoid(f + bf_ref[...])
    lf_ref[...] = lf[:, :N_GROUP_HEADS]
    r = lax.broadcasted_iota(jnp.int32, (tm, tm), 0)
    c = lax.broadcasted_iota(jnp.int32, (tm, tm), 1)
    if prompt:
        @pl.when(pl.program_id(0) % (seq_len // tm) == 0)
        def _():
            carry_ref[...] = jnp.zeros_like(carry_ref)

        tri01 = jnp.where(c <= r, 1.0, 0.0).astype(BF16)
        cs = _mm_exact_lhs(tri01, lf) + carry_ref[...]
        carry_ref[...] = cs[tm - 1:tm, :]
        km = kn[:, GROUP_WIDTH:nw].reshape(tm // MOBA_BLOCK, MOBA_BLOCK, GROUP_WIDTH)
        km_ref[0] = jnp.sum(km, axis=1) * (1.0 / MOBA_BLOCK)
    else:
        tri01 = jnp.where((c <= r) & (r // seq_len == c // seq_len), 1.0, 0.0).astype(BF16)
        cs = _mm_exact_lhs(tri01, lf)
    c_ref[...] = cs[:, :N_GROUP_HEADS]


def _proj(x, lw, *, prompt, seq_len):
    n, d = x.shape
    tm = 512 if prompt else n
    grid = n // tm
    row = lambda w: pl.BlockSpec((tm, w), lambda i: (i, 0))
    in_specs = [row(d), _const_spec((1, d)), _const_spec((d, N_RWKV_COLS)), _const_spec((d, ATT_WIDTH)),
                _const_spec((d, ATT_WIDTH)), _const_spec((d, ATT_WIDTH)), _const_spec((d, LANES)),
                _const_spec((d, LANES)), _const_spec((1, 2 * GROUP_WIDTH)), _const_spec((1, 2 * GROUP_WIDTH)),
                _const_spec((1, LANES))]
    out_shape = [jax.ShapeDtypeStruct((n, N_RWKV_COLS), F32), jax.ShapeDtypeStruct((n, ATT_WIDTH), F32),
                 jax.ShapeDtypeStruct((n, 2 * ATT_WIDTH), F32), jax.ShapeDtypeStruct((n, N_GROUP_HEADS), F32),
                 jax.ShapeDtypeStruct((n, N_GROUP_HEADS), F32)]
    out_specs = [row(N_RWKV_COLS), row(ATT_WIDTH), row(2 * ATT_WIDTH), row(N_GROUP_HEADS), row(N_GROUP_HEADS)]
    scratch = []
    if prompt:
        nb = tm // MOBA_BLOCK
        out_shape += [jax.ShapeDtypeStruct((grid, nb, GROUP_WIDTH), F32)]
        out_specs += [pl.BlockSpec((1, nb, GROUP_WIDTH), lambda i: (i, 0, 0))]
        scratch = [pltpu.VMEM((1, LANES), F32)]
    return pl.pallas_call(
        functools.partial(_proj_kernel, prompt=prompt, tm=tm, seq_len=seq_len),
        grid=(grid,), in_specs=in_specs, out_specs=out_specs, out_shape=out_shape, scratch_shapes=scratch,
        compiler_params=_params(("arbitrary",)), name="proj_prompt" if prompt else "proj_sample",
    )(x, lw["attn_norm"], lw["w_r"], lw["w_q"], lw["w_k"], lw["w_v"], lw["w_f_hi"], lw["w_f_lo"],
      lw["q_norm"], lw["k_norm"], lw["b_forget"])


MLP_FF_CHUNK = 1024


def _mlp_kernel(*refs):
    x_ref, y_refs = refs[0], refs[1:-7]
    wo_ref, g_ref, wu_ref, wd_ref, o_ref, acc_scr, h_scr = refs[-7:]
    c = pl.program_id(1)

    @pl.when(c == 0)
    def _():
        x = x_ref[...]
        off = 0
        for y_ref in y_refs:
            w = y_ref.shape[1]
            x = x + jnp.dot(_bf(y_ref[...]), wo_ref[off:off + w, :], preferred_element_type=F32)
            off += w
        acc_scr[...] = x
        h_scr[...] = _bf(_rms(x, g_ref[...]))

    u = jnp.dot(h_scr[...], wu_ref[...], preferred_element_type=F32)
    u = jnp.square(jnp.maximum(u, 0.0))
    acc_scr[...] += jnp.dot(_bf(u), wd_ref[...], preferred_element_type=F32)

    @pl.when(c == pl.num_programs(1) - 1)
    def _():
        o_ref[...] = acc_scr[...]


def _mlp(x, ys, lw):
    n, d = x.shape
    d_ff = lw["w_up"].shape[1]
    tm = min(512, n)
    fc = MLP_FF_CHUNK
    row = lambda w: pl.BlockSpec((tm, w), lambda i, c: (i, 0))
    return pl.pallas_call(
        _mlp_kernel, grid=(n // tm, d_ff // fc),
        in_specs=([row(d)] + [row(y.shape[1]) for y in ys]
                  + [_const_spec((d, d)), _const_spec((1, d)),
                     pl.BlockSpec((d, fc), lambda i, c: (0, c)), pl.BlockSpec((fc, d), lambda i, c: (c, 0))]),
        out_specs=row(d), out_shape=jax.ShapeDtypeStruct((n, d), F32),
        scratch_shapes=[pltpu.VMEM((tm, d), F32), pltpu.VMEM((tm, d), BF16)],
        compiler_params=_params(("arbitrary", "arbitrary")), name="mlp",
    )(x, *ys, lw["w_out"], lw["mlp_norm"], lw["w_up"], lw["w_down"])


def _rwkv_kernel(p_ref, mu_ref, w0_ref, a0_ref, wwa_ref, gup_ref, kk_ref, ka_ref, rk_ref,
                 gnw_ref, gnb_ref, y_ref, s_scr, carry_scr, *, t_valid, chunk):
    t = pl.program_id(1)
    gsz = p_ref.shape[1]
    gw = GROUP_WIDTH
    p = p_ref[0]
    rows = lax.broadcasted_iota(jnp.int32, (gsz, 1), 0)
    prev = jnp.where(rows == 0, carry_scr[...], pltpu.roll(p, 1, axis=0))
    carry_scr[...] = p[gsz - 1:gsz, :]
    xm = p + (prev - p) * mu_ref[...]
    r, k, v = xm[:, :gw], xm[:, gw:2 * gw], xm[:, 2 * gw:3 * gw]
    xwa = xm[:, 3 * gw:3 * gw + LANES]
    xg = xm[:, 3 * gw + LANES:]
    lane_l = lax.broadcasted_iota(jnp.int32, (1, LANES), 1)
    lora = _mm3(jnp.where(lane_l < LANES // 2, jnp.tanh(xwa), xwa), wwa_ref[...])
    w_log = -_softplus(-(w0_ref[...] + lora[:, :gw])) - RWKV_DECAY_OFFSET
    logw = -jnp.exp(w_log)
    a = _sigmoid(a0_ref[...] + lora[:, gw:])
    g = _mm3(_sigmoid(xg), gup_ref[...])
    seg01 = _seg_ones(gw, HEAD_DIM)
    kk = k * kk_ref[...]
    kk = kk * lax.rsqrt(jnp.maximum(_head_sum(kk * kk, seg01), 1e-24))
    k2 = k * (1.0 + (a - 1.0) * ka_ref[...])
    bonus = _head_sum(r * k2 * rk_ref[...], seg01) * v
    if t_valid is not None:
        ok = (rows + t * gsz) < t_valid
        logw, kk, k2 = jnp.where(ok, logw, 0.0), jnp.where(ok, kk, 0.0), jnp.where(ok, k2, 0.0)
        r, v = jnp.where(ok, r, 0.0), jnp.where(ok, v, 0.0)
    b = kk * a

    ri = lax.broadcasted_iota(jnp.int32, (gsz, gsz), 0)
    ci = lax.broadcasted_iota(jnp.int32, (gsz, gsz), 1)
    same = (ri // chunk) == (ci // chunk)
    strict = same & (ci < ri)
    incl = same & (ci <= ri)
    lc = _mm_exact_lhs(jnp.where(incl, 1.0, 0.0).astype(BF16), logw)
    lcc = _mm_exact_lhs(jnp.where(same, 1.0, 0.0).astype(BF16), logw)
    e_neg = jnp.exp(-lc)
    e_end = jnp.exp(lcc - lc)
    at = -kk * jnp.exp(lc - logw)
    bt, kt = b * e_neg, k2 * e_neg
    rt = r * jnp.exp(lc)
    bd, kd = b * e_end, k2 * e_end
    g_end = jnp.exp(lcc)

    lane = lax.broadcasted_iota(jnp.int32, (1, gw), 1) // HEAD_DIM
    eye = jnp.where(ri == ci, 1.0, 0.0)
    rhs = jnp.concatenate([bt, kt], axis=0)
    n_sq = chunk.bit_length() - 2
    heads = range(N_GROUP_HEADS)
    masks = [lane == h for h in heads]
    grams = [_mm_nt(jnp.concatenate([jnp.where(mh, at, 0.0), jnp.where(mh, rt, 0.0)], axis=0), rhs)
             for mh in masks]
    ab = [jnp.where(strict, gm[:gsz, :gsz], 0.0) for gm in grams]
    ak = [jnp.where(strict, gm[:gsz, gsz:], 0.0) for gm in grams]
    rbs = [jnp.where(incl, gm[gsz:, :gsz], 0.0) for gm in grams]
    rk = [jnp.where(incl, gm[gsz:, gsz:], 0.0) for gm in grams]
    akv = [_mm(jnp.concatenate([ak[h], rk[h]], axis=0), v) for h in heads]
    tinv = [eye + x for x in ab]
    pw = ab
    for _ in range(n_sq):
        pw = [_mm(x, x) for x in pw]
        tinv = [tinv[h] + _mm(tinv[h], pw[h]) for h in heads]
    tw = [_mm(tinv[h], jnp.concatenate([at, akv[h][:gsz]], axis=1)) for h in heads]
    wm = _merge_lanes([x[:, :gw] for x in tw], masks)
    u0 = _merge_lanes([x[:, gw:] for x in tw], masks)
    y0 = _merge_lanes([x[gsz:] for x in akv], masks)
    yield

    r2 = lax.broadcasted_iota(jnp.int32, (gw, gw), 0) // HEAD_DIM
    c2 = lax.broadcasted_iota(jnp.int32, (gw, gw), 1) // HEAD_DIM
    blockdiag = r2 == c2
    s = s_scr[...]
    us, zs = [], []
    for c in range(gsz // chunk):
        sl = slice(c * chunk, (c + 1) * chunk)
        z = _mm_nt(jnp.concatenate([wm[sl], rt[sl]], axis=0), s)
        u = z[:chunk] + u0[sl]
        us.append(u)
        zs.append(z[chunk:])
        upd = _mm_tn(jnp.concatenate([u, v[sl]], axis=0), jnp.concatenate([bd[sl], kd[sl]], axis=0))
        s = jnp.where(blockdiag, s * g_end[c * chunk:c * chunk + 1, :] + upd, 0.0)
        yield
    s_scr[...] = s
    u_all = jnp.concatenate(us, axis=0)
    y = jnp.concatenate(zs, axis=0) + y0 + _merge_lanes([_mm(rbs[h], u_all) for h in heads], masks)

    mean = _head_sum(y, seg01) * (1.0 / HEAD_DIM)
    yc = y - mean
    var = _head_sum(yc * yc, seg01) * (1.0 / HEAD_DIM)
    yn = yc * lax.rsqrt(var + RWKV_GN_EPS) * gnw_ref[...] + gnb_ref[...]
    y_ref[0] = (yn + bonus) * g
    yield s


def _merge_lanes(parts, masks):
    out = parts[-1]
    for x, m in zip(parts[-2::-1], masks[-2::-1]):
        out = jnp.where(m, x, out)
    return out


def _rwkv_multi_kernel(p_ref, sh0_ref, s0_ref, *refs, t_valid, chunk):
    weights, (y_ref, sfin_ref, s_scr, carry_scr) = refs[:-4], refs[-4:]
    t = pl.program_id(1)

    @pl.when(t == 0)
    def _():
        carry_scr[...] = sh0_ref[...]
        s_scr[...] = s0_ref[...]

    gens = [_rwkv_kernel(p_ref.at[pl.ds(i, 1)], *weights, y_ref.at[pl.ds(i, 1)], s_scr.at[i], carry_scr.at[i],
                         t_valid=t_valid, chunk=chunk) for i in range(p_ref.shape[0])]
    states = [None] * len(gens)
    for _ in range(p_ref.shape[1] // chunk + 2):
        for i, gen in enumerate(gens):
            states[i] = next(gen)

    @pl.when(t == pl.num_programs(1) - 1)
    def _():
        for i, s in enumerate(states):
            sfin_ref[i] = s


def _blockdiag_state(s):
    b = s.shape[0]
    eye = jnp.eye(N_GROUP_HEADS, dtype=s.dtype)
    return jnp.einsum("bhij,hg->bhigj", s, eye).reshape(b, GROUP_WIDTH, GROUP_WIDTH)


def _rwkv(p, shift0, s0, lw):
    bsz, t, _ = p.shape
    gsz = RWKV_GROUP
    t_pad = -(-t // gsz) * gsz
    if t_pad != t:
        p = jnp.pad(p, ((0, 0), (0, t_pad - t), (0, 0)))
    gw = GROUP_WIDTH
    vec = lambda w: _const_spec((1, w))
    nb = RWKV_SEQS_PER_STEP
    y, s_bd = pl.pallas_call(
        functools.partial(_rwkv_multi_kernel, t_valid=None if t_pad == t else t, chunk=RWKV_CHUNK),
        grid=(bsz // nb, t_pad // gsz),
        in_specs=[pl.BlockSpec((nb, gsz, N_RWKV_COLS), lambda b, i: (b, i, 0)),
                  pl.BlockSpec((nb, 1, N_RWKV_COLS), lambda b, i: (b, 0, 0)),
                  pl.BlockSpec((nb, gw, gw), lambda b, i: (b, 0, 0)),
                  vec(N_RWKV_COLS), vec(gw), vec(gw), _const_spec((LANES, 2 * gw)), _const_spec((LANES, gw)),
                  vec(gw), vec(gw), vec(gw), vec(gw), vec(gw)],
        out_specs=[pl.BlockSpec((nb, gsz, gw), lambda b, i: (b, i, 0)),
                   pl.BlockSpec((nb, gw, gw), lambda b, i: (b, 0, 0))],
        out_shape=[jax.ShapeDtypeStruct((bsz, t_pad, gw), F32), jax.ShapeDtypeStruct((bsz, gw, gw), F32)],
        scratch_shapes=[pltpu.VMEM((nb, gw, gw), F32), pltpu.VMEM((nb, 1, N_RWKV_COLS), F32)],
        compiler_params=_params(("arbitrary", "arbitrary")), name="rwkv",
    )(p, shift0.reshape(bsz, 1, N_RWKV_COLS), _blockdiag_state(s0), lw["mu"], lw["w0"], lw["a0"], lw["w_wa"],
      lw["g_up"], lw["k_k"], lw["k_a"], lw["r_k"], lw["gn_w"], lw["gn_b"])
    s_fin = s_bd.reshape(bsz, N_GROUP_HEADS, HEAD_DIM, N_GROUP_HEADS, HEAD_DIM)
    s_fin = jnp.stack([s_fin[:, h, :, h, :] for h in range(N_GROUP_HEADS)], axis=1)
    return y[:, :t], s_fin


ATT_TILE = MOBA_BLOCK


def _head_lane(width=GROUP_WIDTH):
    return lax.broadcasted_iota(jnp.int32, (1, width), 1) // HEAD_DIM


LOG2E = 1.4426950408889634
SB_DEAD = -104.0


def _lane_bcast(col, width=LANES):
    return jnp.broadcast_to(col, (col.shape[0], width))


def _twice(x):
    return jnp.concatenate([x, x], axis=1)


def _merge_heads(parts, low):
    left = jnp.where(low, parts[0][:, :LANES], parts[1][:, :LANES])
    right = jnp.where(low, parts[2][:, LANES:], parts[3][:, LANES:])
    return jnp.concatenate([left, right], axis=1)


def _flash2_kernel(*refs, mode, n_blocks):
    if mode == "fox":
        q_ref, k_ref, v_ref, cq_ref, ck_ref, o_ref, qs_scr, m_scr, l_scr, acc_scr, cq_scr = refs
    else:
        q_ref, k_ref, v_ref, km_ref, o_ref, qs_scr, m_scr, l_scr, acc_scr, sel_scr = refs
    qi = pl.program_id(1)
    ki = pl.program_id(2)
    tq, tk = q_ref.shape[0], k_ref.shape[0]
    nh = N_GROUP_HEADS
    lane = _head_lane()
    low = lax.broadcasted_iota(jnp.int32, (1, LANES), 1) < HEAD_DIM

    @pl.when(ki == 0)
    def _():
        m_scr[...] = jnp.full(m_scr.shape, NEG_BIG, F32)
        l_scr[...] = jnp.zeros_like(l_scr)
        acc_scr[...] = jnp.zeros_like(acc_scr)
        q = q_ref[...]
        qs = q * (QK_SCALE * LOG2E)
        for h in range(nh):
            qs_scr[h * tq:(h + 1) * tq, :] = _bf(jnp.where(lane == h, qs, 0.0))
        if mode == "fox":
            cq = cq_ref[...] * LOG2E
            for h in range(nh):
                cq_scr[h * tq:(h + 1) * tq, :] = _lane_bcast(cq[:, h:h + 1])
        else:
            km = km_ref[0]
            nb8 = -(-n_blocks // 8) * 8
            blk = lax.broadcasted_iota(jnp.int32, (nb8, 1), 0)
            for h in range(nh):
                gate = _mm3_nt(km, jnp.where(lane == h, q, 0.0))[:nb8]
                rank = jnp.zeros((nb8, tq), F32)
                for m in range(n_blocks):
                    gm = gate[m:m + 1, :]
                    tie = jnp.where(m < blk, 1.0, 0.0)
                    beats = jnp.where(gm > gate, 1.0, jnp.where(gm == gate, tie, 0.0))
                    rank = rank + beats * jnp.where(m < qi, 1.0, 0.0)
                sel = jnp.where(blk < qi, jnp.where(rank < MOBA_TOPK, 1.0, 0.0), 0.0)
                sel = jnp.concatenate([sel, jnp.zeros((LANES - nb8, tq), F32)], axis=0)
                sel_scr[h] = sel.T

    def step(diag):
        k = _bf(k_ref[...])
        v = _bf(v_ref[...])
        s_all = lax.dot_general(qs_scr[...], k, (((1,), (1,)), ((), ())), preferred_element_type=F32)
        if diag:
            causal = (lax.broadcasted_iota(jnp.int32, (1, tk), 1) <= lax.broadcasted_iota(jnp.int32, (tq, 1), 0))
        probs, alphas = [], []
        for h in range(nh):
            rows = slice(h * tq, (h + 1) * tq)
            s = s_all[rows]
            if mode == "fox":
                s = s - ck_ref[0, h:h + 1, :] * LOG2E
            if diag:
                s = jnp.where(causal, s, NEG_BIG)
            m_old = m_scr[rows]
            rmax = _lane_bcast(jnp.max(s, axis=1, keepdims=True))
            if mode == "fox":
                cq = cq_scr[rows]
                m_new = jnp.maximum(m_old, rmax + cq)
                shift = m_new - cq
            elif diag:
                m_new = jnp.maximum(m_old, rmax)
                shift = m_new
            else:
                col = lax.broadcasted_iota(jnp.int32, (1, LANES), 1)
                picked = _lane_bcast(jnp.sum(jnp.where(col == ki, sel_scr[h], 0.0), axis=1, keepdims=True)) > 0.5
                m_new = jnp.where(picked, jnp.maximum(m_old, rmax), m_old)
                shift = jnp.where(picked, m_new, -NEG_BIG)
            pr = jnp.exp2(s - _twice(shift))
            alpha = jnp.exp2(m_old - m_new)
            l_scr[rows] = alpha * l_scr[rows] + _lane_bcast(jnp.sum(pr, axis=1, keepdims=True))
            m_scr[rows] = m_new
            probs.append(_bf(pr))
            alphas.append(_twice(alpha))
        pv = jnp.dot(jnp.concatenate(probs, axis=0), v, preferred_element_type=F32)
        acc_scr[...] = (acc_scr[...] * _merge_heads(alphas, low)
                        + _merge_heads([pv[h * tq:(h + 1) * tq] for h in range(nh)], low))

    @pl.when(ki < qi)
    def _():
        step(False)

    @pl.when(ki == qi)
    def _():
        step(True)

    @pl.when(ki == pl.num_programs(2) - 1)
    def _():
        inv = _merge_heads([_twice(1.0 / l_scr[h * tq:(h + 1) * tq]) for h in range(nh)], low)
        o_ref[...] = acc_scr[...] * inv


def _flash2(q, kv, *, mode, batch, seq_len, q_col, c=None, c_t=None, kmean=None):
    n = q.shape[0]
    t = ATT_TILE
    nq = seq_len // t
    gw = GROUP_WIDTH
    nh = N_GROUP_HEADS
    v_col = q_col + ATT_WIDTH // gw
    in_specs = [pl.BlockSpec((t, gw), lambda b, i, j: (b * nq + i, q_col)),
                pl.BlockSpec((t, gw), lambda b, i, j: (b * nq + jnp.minimum(i, j), q_col)),
                pl.BlockSpec((t, gw), lambda b, i, j: (b * nq + jnp.minimum(i, j), v_col))]
    scratch = [pltpu.VMEM((nh * t, gw), BF16), pltpu.VMEM((nh * t, LANES), F32), pltpu.VMEM((nh * t, LANES), F32),
               pltpu.VMEM((t, gw), F32)]
    if mode == "fox":
        in_specs += [pl.BlockSpec((t, nh), lambda b, i, j: (b * nq + i, 0)),
                     pl.BlockSpec((1, nh, t), lambda b, i, j: (b, 0, jnp.minimum(i, j)))]
        scratch += [pltpu.VMEM((nh * t, LANES), F32)]
        args = (q, kv, kv, c, c_t)
    else:
        in_specs += [pl.BlockSpec((1, LANES, gw), lambda b, i, j: (b, 0, 0))]
        scratch += [pltpu.VMEM((nh, t, LANES), F32)]
        args = (q, kv, kv, kmean)
    return pl.pallas_call(
        functools.partial(_flash2_kernel, mode=mode, n_blocks=nq),
        grid=(batch, nq, nq), in_specs=in_specs,
        out_specs=pl.BlockSpec((t, gw), lambda b, i, j: (b * nq + i, 0)),
        out_shape=jax.ShapeDtypeStruct((n, gw), F32), scratch_shapes=scratch,
        compiler_params=_params(("arbitrary", "arbitrary", "arbitrary")), name="attn_" + mode,
    )(*args)


def _sb2_kernel(q_ref, k_ref, v_ref, o_ref, qs_scr, r_scr, acc_scr, dead_ref):
    qi = pl.program_id(1)
    step_id = pl.program_id(2)
    tq, tk = q_ref.shape[0], k_ref.shape[0]
    nh = N_GROUP_HEADS
    lane = _head_lane()
    low = lax.broadcasted_iota(jnp.int32, (1, LANES), 1) < HEAD_DIM

    @pl.when(step_id == 0)
    def _():
        r_scr[...] = jnp.zeros_like(r_scr)
        acc_scr[...] = jnp.zeros_like(acc_scr)
        dead_ref[0] = 0
        qs = q_ref[...] * QK_SCALE
        for h in range(nh):
            qs_scr[h * tq:(h + 1) * tq, :] = _bf(jnp.where(lane == h, qs, 0.0))

    def step(diag):
        k = _bf(k_ref[...])
        v = _bf(v_ref[...])
        z = lax.dot_general(qs_scr[...], k, (((1,), (1,)), ((), ())), preferred_element_type=F32)
        soft = jnp.maximum(z, 0.0) + jnp.log(1.0 + jnp.exp(-jnp.abs(z)))
        log_beta = z - soft
        keep = -soft
        if diag:
            qrow = lax.broadcasted_iota(jnp.int32, (nh * tq, 1), 0) % tq
            strict = lax.broadcasted_iota(jnp.int32, (1, tk), 1) < qrow
            keep = jnp.where(strict, keep, 0.0)
        later01 = jnp.where(lax.broadcasted_iota(jnp.int32, (tk, tk), 0) > lax.broadcasted_iota(jnp.int32, (tk, tk), 1),
                            1.0, 0.0).astype(BF16)
        r_old = r_scr[...]
        att = jnp.exp(log_beta + _mm_exact_rhs(keep, later01) + _twice(r_old))
        if diag:
            att = jnp.where(strict, att, 0.0)
        pv = jnp.dot(_bf(att), v, preferred_element_type=F32)
        acc_scr[...] += _merge_heads([pv[h * tq:(h + 1) * tq] for h in range(nh)], low)
        r_new = r_old + _lane_bcast(jnp.sum(keep, axis=1, keepdims=True))
        r_scr[...] = r_new
        dead_ref[0] = (jnp.max(r_new) < SB_DEAD).astype(jnp.int32)

    @pl.when(step_id == 0)
    def _():
        step(True)

    @pl.when((step_id > 0) & (step_id <= qi) & (dead_ref[0] == 0))
    def _():
        step(False)

    @pl.when(step_id == pl.num_programs(2) - 1)
    def _():
        o_ref[...] = acc_scr[...]


def _sb2(q, kv, *, batch, seq_len, q_col):
    n = q.shape[0]
    t = ATT_TILE
    nq = seq_len // t
    gw = GROUP_WIDTH
    nh = N_GROUP_HEADS
    v_col = q_col + ATT_WIDTH // gw
    kblk = lambda b, i, j: b * nq + jnp.maximum(i - j, 0)
    return pl.pallas_call(
        _sb2_kernel, grid=(batch, nq, nq),
        in_specs=[pl.BlockSpec((t, gw), lambda b, i, j: (b * nq + i, q_col)),
                  pl.BlockSpec((t, gw), lambda b, i, j: (kblk(b, i, j), q_col)),
                  pl.BlockSpec((t, gw), lambda b, i, j: (kblk(b, i, j), v_col))],
        out_specs=pl.BlockSpec((t, gw), lambda b, i, j: (b * nq + i, 0)),
        out_shape=jax.ShapeDtypeStruct((n, gw), F32),
        scratch_shapes=[pltpu.VMEM((nh * t, gw), BF16), pltpu.VMEM((nh * t, LANES), F32), pltpu.VMEM((t, gw), F32),
                        pltpu.SMEM((1,), jnp.int32)],
        compiler_params=_params(("arbitrary", "arbitrary", "arbitrary")), name="attn_sb",
    )(q, kv, kv)


PAGE = 128
LOGF_PAGES_PER_STEP = 16


def _foxpast_kernel(pt_ref, *refs):
    n_grp = LOGF_PAGES_PER_STEP
    page_refs, o_ref, x_scr, carry_scr = refs[:n_grp], refs[n_grp], refs[n_grp + 1], refs[n_grp + 2]
    del pt_ref

    @pl.when(pl.program_id(1) == 0)
    def _():
        carry_scr[...] = jnp.zeros_like(carry_scr)

    for i in range(n_grp):
        x_scr[i:i + 1, :] = page_refs[i][0]
    x = x_scr[...]
    nh = N_GROUP_HEADS
    li = lax.broadcasted_iota(jnp.int32, (PAGE * nh, PAGE), 0)
    ki = lax.broadcasted_iota(jnp.int32, (PAGE * nh, PAGE), 1)
    later01 = jnp.where(lax.broadcasted_iota(jnp.int32, (PAGE, PAGE), 0) > lax.broadcasted_iota(jnp.int32, (PAGE, PAGE), 1),
                        1.0, 0.0).astype(BF16)
    later_pg01 = jnp.where(lax.broadcasted_iota(jnp.int32, (n_grp, n_grp), 1) > lax.broadcasted_iota(jnp.int32, (n_grp, n_grp), 0),
                           1.0, 0.0).astype(BF16)
    for h in range(nh):
        pick01 = jnp.where((li // nh == ki) & (li % nh == h), 1.0, 0.0).astype(BF16)
        lh = _mm_exact_rhs(x, pick01)
        within = _mm_exact_rhs(lh, later01)
        later_pages = jnp.sum(_mm_exact_lhs(later_pg01, lh), axis=1, keepdims=True)
        total = jnp.sum(jnp.sum(lh, axis=1, keepdims=True), axis=0, keepdims=True)
        carry = carry_scr[h:h + 1, :]
        o_ref[0, h] = within + later_pages + carry
        carry_scr[h:h + 1, :] = carry + total


def _fox_past(page_table, logf_pages, page_offset):
    bsz, n_pages = page_table.shape
    n_grp = LOGF_PAGES_PER_STEP
    groups = n_pages // n_grp

    def page_spec(i):
        return pl.BlockSpec((1, 1, PAGE * N_GROUP_HEADS),
                            lambda b, g, pt: (page_offset + pt[b * n_pages + (groups - 1 - g) * n_grp + i], 0, 0))

    out = pl.pallas_call(
        _foxpast_kernel,
        grid_spec=pltpu.PrefetchScalarGridSpec(
            num_scalar_prefetch=1, grid=(bsz, groups),
            in_specs=[page_spec(i) for i in range(n_grp)],
            out_specs=pl.BlockSpec((1, N_GROUP_HEADS, n_grp, PAGE), lambda b, g, pt: (b, 0, groups - 1 - g, 0)),
            scratch_shapes=[pltpu.VMEM((n_grp, PAGE * N_GROUP_HEADS), F32), pltpu.VMEM((8, PAGE), F32)]),
        out_shape=jax.ShapeDtypeStruct((bsz, N_GROUP_HEADS, n_pages, PAGE), F32),
        compiler_params=_params(("arbitrary", "arbitrary")), name="fox_past",
    )(page_table.reshape(-1), *([logf_pages] * n_grp))
    return out.reshape(bsz, N_GROUP_HEADS, n_pages * PAGE)


N_MIXERS = ATT_WIDTH // GROUP_WIDTH


def _decode2_kernel(pt_ref, q_ref, kvn_ref, cn_ref, cnt_ref, sfx_ref, pg0_ref, pg1_ref, o_ref,
                    qbd_scr, cn_scr, fm_scr, fl_scr, facc_scr, sr_scr, sacc_scr, mg_scr, mm_scr, ml_scr, mo_scr,
                    om_scr, ol_scr, oo_scr, *, n_blocks):
    del pt_ref
    s_id = pl.program_id(1)
    nq = q_ref.shape[0]
    nh = N_GROUP_HEADS
    gw = GROUP_WIDTH
    rows = nh * nq
    lane = lax.broadcasted_iota(jnp.int32, (1, LANES), 1)
    row_head = lax.broadcasted_iota(jnp.int32, (rows, 1), 0) // nq
    qrow = lax.broadcasted_iota(jnp.int32, (rows, 1), 0) % nq
    own_lanes = row_head == _head_lane()

    def stack_heads(x):
        return jnp.concatenate([x] * nh, axis=0)

    @pl.when(s_id == 0)
    def _():
        for m in range(N_MIXERS):
            qm = stack_heads(q_ref[:, m * gw:(m + 1) * gw])
            qbd_scr[m] = jnp.where(own_lanes, qm, 0.0)
        cn = cn_ref[...]
        cn_scr[...] = jnp.concatenate([_lane_bcast(cn[:, h:h + 1]) for h in range(nh)], axis=0)
        fm_scr[...] = jnp.full(fm_scr.shape, NEG_BIG, F32)
        for ref in (fl_scr, facc_scr, sr_scr, sacc_scr, mg_scr, mm_scr, ml_scr):
            ref[...] = jnp.zeros_like(ref)

    def attend(k_tile, v_tile, fox_rows, mask_incl, mask_strict, block):
        nk = k_tile.shape[0]
        kb = _bf(k_tile)
        vb = _bf(v_tile)
        dn = (((1,), (1,)), ((), ()))
        z = [lax.dot_general(_bf(qbd_scr[m] * QK_SCALE), kb[:, m * gw:(m + 1) * gw], dn, preferred_element_type=F32)
             for m in range(N_MIXERS)]
        vm = [vb[:, m * gw:(m + 1) * gw] for m in range(N_MIXERS)]

        bias = jnp.concatenate([jnp.broadcast_to(fox_rows[h:h + 1, :], (nq, nk)) for h in range(nh)], axis=0)
        sc = z[0] + bias
        if mask_incl is not None:
            sc = jnp.where(mask_incl, sc, NEG_BIG)
        cn = cn_scr[...]
        m_old = fm_scr[...]
        m_new = jnp.maximum(m_old, _lane_bcast(jnp.max(sc, axis=1, keepdims=True)) + cn)
        pr = jnp.exp(sc - _lane_bcast((m_new - cn)[:, :1], nk))
        alpha = jnp.exp(m_old - m_new)
        fl_new = alpha * fl_scr[...] + _lane_bcast(jnp.sum(pr, axis=1, keepdims=True))
        facc_new = _twice(alpha) * facc_scr[...] + jnp.dot(_bf(pr), vm[0], preferred_element_type=F32)

        sc = z[1] if mask_incl is None else jnp.where(mask_incl, z[1], NEG_BIG)
        m_b = _lane_bcast(jnp.max(sc, axis=1, keepdims=True))
        pr = jnp.exp(sc - _lane_bcast(m_b[:, :1], nk))
        l_b = _lane_bcast(jnp.sum(pr, axis=1, keepdims=True))
        o_b = jnp.dot(_bf(pr), vm[1], preferred_element_type=F32)

        soft = jnp.maximum(z[2], 0.0) + jnp.log(1.0 + jnp.exp(-jnp.abs(z[2])))
        keep = -soft
        if mask_strict is not None:
            keep = jnp.where(mask_strict, keep, 0.0)
        later01 = jnp.where(lax.broadcasted_iota(jnp.int32, (nk, nk), 0) > lax.broadcasted_iota(jnp.int32, (nk, nk), 1),
                            1.0, 0.0).astype(BF16)
        r_old = sr_scr[...]
        att = jnp.exp(z[2] - soft + _mm_exact_rhs(keep, later01) + _lane_bcast(r_old[:, :1], nk))
        if mask_strict is not None:
            att = jnp.where(mask_strict, att, 0.0)
        sacc_new = sacc_scr[...] + jnp.dot(_bf(att), vm[2], preferred_element_type=F32)
        sr_new = r_old + _lane_bcast(jnp.sum(keep, axis=1, keepdims=True))

        fm_scr[...] = m_new
        fl_scr[...] = fl_new
        facc_scr[...] = facc_new
        sacc_scr[...] = sacc_new
        sr_scr[...] = sr_new
        if block is None:
            om_scr[...], ol_scr[...], oo_scr[...] = m_b, l_b, o_b
        else:
            kmean = jnp.sum(k_tile[:, gw:2 * gw], axis=0, keepdims=True) * (1.0 / MOBA_BLOCK)
            gate = _lane_bcast(jnp.sum(qbd_scr[1] * kmean, axis=1, keepdims=True))
            here = lane == block
            mg_scr[...] = jnp.where(here, gate, mg_scr[...])
            mm_scr[...] = jnp.where(here, m_b, mm_scr[...])
            ml_scr[...] = jnp.where(here, l_b, ml_scr[...])
            mo_scr[pl.ds(block, 1)] = o_b[None]

    @pl.when(s_id == 0)
    def _():
        pad = jnp.zeros((PAGE - nq, 2 * ATT_WIDTH), F32)
        kvn = jnp.concatenate([kvn_ref[...], pad], axis=0)
        key = lax.broadcasted_iota(jnp.int32, (1, PAGE), 1)
        attend(kvn[:, :ATT_WIDTH], kvn[:, ATT_WIDTH:], -cnt_ref[0], key <= qrow, key < qrow, None)

    block = n_blocks - 1 - s_id
    k_tile = jnp.concatenate([pg0_ref[0, :, :ATT_WIDTH], pg1_ref[0, :, :ATT_WIDTH]], axis=0)
    v_tile = jnp.concatenate([pg0_ref[0, :, ATT_WIDTH:], pg1_ref[0, :, ATT_WIDTH:]], axis=0)
    attend(k_tile, v_tile, sfx_ref[0], None, None, block)

    @pl.when(s_id == pl.num_programs(1) - 1)
    def _():
        def store(m, res):
            out = jnp.zeros((nq, gw), F32)
            for h in range(nh):
                out = jnp.where(_head_lane() == h, res[h * nq:(h + 1) * nq], out)
            o_ref[:, m * gw:(m + 1) * gw] = out

        store(0, facc_scr[...] / _twice(fl_scr[...]))
        store(2, sacc_scr[...])
        gate, m_all, l_all = mg_scr[...], mm_scr[...], ml_scr[...]
        rank = jnp.zeros((rows, LANES), F32)
        for n in range(n_blocks):
            gn = gate[:, n:n + 1]
            tie = jnp.where(n < lane, 1.0, 0.0)
            rank = rank + jnp.where(gn > gate, 1.0, jnp.where(gn == gate, tie, 0.0))
        sel = jnp.where(lane < n_blocks, jnp.where(rank < MOBA_TOPK, 1.0, 0.0), 0.0) > 0.5
        m_own = om_scr[...]
        m_top = jnp.maximum(_lane_bcast(jnp.max(jnp.where(sel, m_all, NEG_BIG), axis=1, keepdims=True)), m_own)
        w = jnp.where(sel, jnp.exp(m_all - m_top), 0.0)
        w_own = jnp.exp(m_own - m_top)
        den = _lane_bcast(jnp.sum(w * l_all, axis=1, keepdims=True)) + w_own * ol_scr[...]
        num = _twice(w_own) * oo_scr[...]
        for n in range(n_blocks):
            num = num + _lane_bcast(w[:, n:n + 1], gw) * mo_scr[n]
        store(1, num / _twice(den))


def _decode2(page_table, q, kv_new, c_new, sfx, kv_pages, page_offset):
    bsz, n_pages = page_table.shape
    nq = q.shape[0] // bsz
    nh = N_GROUP_HEADS
    gw = GROUP_WIDTH
    n_blocks = n_pages * PAGE // MOBA_BLOCK
    assert MOBA_BLOCK == 2 * PAGE and n_blocks <= LANES
    cnt = jnp.pad(jnp.swapaxes(c_new.reshape(bsz, nq, nh), 1, 2), ((0, 0), (0, 0), (0, PAGE - nq)))

    def page_spec(r):
        return pl.BlockSpec((1, PAGE, 2 * ATT_WIDTH),
                            lambda b, s, pt: (page_offset + pt[b * n_pages + 2 * (n_blocks - 1 - s) + r], 0, 0))

    rows = nh * nq
    stat = lambda: pltpu.VMEM((rows, LANES), F32)
    wide = lambda: pltpu.VMEM((rows, gw), F32)
    return pl.pallas_call(
        functools.partial(_decode2_kernel, n_blocks=n_blocks),
        grid_spec=pltpu.PrefetchScalarGridSpec(
            num_scalar_prefetch=1, grid=(bsz, n_blocks),
            in_specs=[pl.BlockSpec((nq, ATT_WIDTH), lambda b, s, pt: (b, 0)),
                      pl.BlockSpec((nq, 2 * ATT_WIDTH), lambda b, s, pt: (b, 0)),
                      pl.BlockSpec((nq, nh), lambda b, s, pt: (b, 0)),
                      pl.BlockSpec((1, nh, PAGE), lambda b, s, pt: (b, 0, 0)),
                      pl.BlockSpec((1, nh, MOBA_BLOCK), lambda b, s, pt: (b, 0, n_blocks - 1 - s)),
                      page_spec(0), page_spec(1)],
            out_specs=pl.BlockSpec((nq, ATT_WIDTH), lambda b, s, pt: (b, 0)),
            scratch_shapes=[pltpu.VMEM((N_MIXERS, rows, gw), F32), stat(), stat(), stat(), wide(), stat(), wide(),
                            stat(), stat(), stat(), pltpu.VMEM((n_blocks, rows, gw), F32), stat(), stat(), wide()]),
        out_shape=jax.ShapeDtypeStruct((bsz * nq, ATT_WIDTH), F32),
        compiler_params=_params(("arbitrary", "arbitrary")), name="decode_attn",
    )(page_table.reshape(-1), q, kv_new, c_new, cnt, sfx, kv_pages, kv_pages)


def _layer_weights(l, attn_norm, w_in, rwkv_mu, rwkv_w0, rwkv_w_up, rwkv_a0, rwkv_a_up, rwkv_g_up,
                   rwkv_k_k, rwkv_k_a, rwkv_r_k, rwkv_gn_w, rwkv_gn_b, q_norm, k_norm, b_forget,
                   w_out, mlp_norm, w_mlp_up, w_mlp_down):
    w = w_in[l]
    a0, a1, a2, a3 = N_RWKV_COLS, N_RWKV_COLS + ATT_WIDTH, N_RWKV_COLS + 2 * ATT_WIDTH, N_RWKV_COLS + 3 * ATT_WIDTH
    w_f = jnp.pad(w[:, a3:], ((0, 0), (0, LANES - N_GROUP_HEADS)))
    w_f_hi = w_f.astype(BF16)
    rank = rwkv_w_up.shape[1]
    z = jnp.zeros((rank, GROUP_WIDTH), F32)
    w_wa = jnp.concatenate([jnp.concatenate([rwkv_w_up[l], z], axis=1),
                            jnp.concatenate([z, rwkv_a_up[l]], axis=1)], axis=0)
    row = lambda v: v.reshape(1, -1)
    return dict(
        attn_norm=row(attn_norm[l]), w_r=w[:, :a0].astype(BF16), w_q=w[:, a0:a1].astype(BF16),
        w_k=w[:, a1:a2].astype(BF16), w_v=w[:, a2:a3].astype(BF16), w_f_hi=w_f_hi,
        w_f_lo=(w_f - w_f_hi.astype(F32)).astype(BF16), q_norm=row(q_norm[l]), k_norm=row(k_norm[l]),
        b_forget=jnp.pad(row(b_forget[l]), ((0, 0), (0, LANES - N_GROUP_HEADS))),
        mu=row(rwkv_mu[l]), w0=row(rwkv_w0[l]), a0=row(rwkv_a0[l]), w_wa=w_wa, g_up=rwkv_g_up[l],
        k_k=row(rwkv_k_k[l]), k_a=row(rwkv_k_a[l]), r_k=row(rwkv_r_k[l]), gn_w=row(rwkv_gn_w[l]),
        gn_b=row(rwkv_gn_b[l]), w_out=w_out[l].astype(BF16), mlp_norm=row(mlp_norm[l]),
        w_up=w_mlp_up[l].astype(BF16), w_down=w_mlp_down[l].astype(BF16))


def _decoder_layer(x, shift0, s0, lw, past):
    bsz, t, d = x.shape
    n = bsz * t
    xf = x.reshape(n, d)
    prompt = past is None
    outs = _proj(xf, lw, prompt=prompt, seq_len=t)
    p, q, kv, lf, c = outs[:5]
    p3 = p.reshape(bsz, t, N_RWKV_COLS)
    y_rwkv, s_new = _rwkv(p3, shift0, s0, lw)
    if prompt:
        kmean = outs[5].reshape(bsz, t // MOBA_BLOCK, GROUP_WIDTH)
        kmean = jnp.pad(kmean, ((0, 0), (0, LANES - t // MOBA_BLOCK), (0, 0)))
        c_t = jnp.swapaxes(c.reshape(bsz, t, N_GROUP_HEADS), 1, 2)
        ys = [_flash2(q, kv, mode="fox", batch=bsz, seq_len=t, q_col=0, c=c, c_t=c_t),
              _flash2(q, kv, mode="moba", batch=bsz, seq_len=t, q_col=1, kmean=kmean),
              _sb2(q, kv, batch=bsz, seq_len=t, q_col=2)]
    else:
        page_table, kv_pages, logf_pages, page_offset = past
        sfx = _fox_past(page_table, logf_pages, page_offset)
        ys = [_decode2(page_table, q, kv, c, sfx, kv_pages, page_offset)]
    x_new = _mlp(xf, [y_rwkv.reshape(n, GROUP_WIDTH)] + ys, lw)
    kv_new = kv.reshape(bsz, t, 2, ATT_WIDTH // HEAD_DIM, HEAD_DIM)
    return x_new.reshape(bsz, t, d), kv_new, lf.reshape(bsz, t, N_GROUP_HEADS), s_new, p3[:, -1]


def kernel(x_prompt, x_sample, cache_kv, cache_logf, state_wkv, state_shift, page_table, attn_norm, w_in, rwkv_mu,
           rwkv_w0, rwkv_w_up, rwkv_a0, rwkv_a_up, rwkv_g_up, rwkv_k_k, rwkv_k_a, rwkv_r_k, rwkv_gn_w, rwkv_gn_b,
           q_norm, k_norm, b_forget, w_out, mlp_norm, w_mlp_up, w_mlp_down):
    depth, n_pool = cache_kv.shape[:2]
    bp = x_prompt.shape[0]
    kv_pages = cache_kv.reshape(depth * n_pool, PAGE, 2 * ATT_WIDTH)
    logf_pages = cache_logf.reshape(depth * n_pool, 1, PAGE * N_GROUP_HEADS)
    yp, ys = x_prompt, x_sample
    outs = [[] for _ in range(8)]
    for l in range(depth):
        lw = _layer_weights(l, attn_norm, w_in, rwkv_mu, rwkv_w0, rwkv_w_up, rwkv_a0, rwkv_a_up, rwkv_g_up,
                            rwkv_k_k, rwkv_k_a, rwkv_r_k, rwkv_gn_w, rwkv_gn_b, q_norm, k_norm, b_forget,
                            w_out, mlp_norm, w_mlp_up, w_mlp_down)
        shift0 = jnp.zeros((bp, N_RWKV_COLS), yp.dtype)
        s0 = jnp.zeros((bp, N_GROUP_HEADS, HEAD_DIM, HEAD_DIM), yp.dtype)
        yp, kvp, lfp, sp, shp = _decoder_layer(yp, shift0, s0, lw, None)
        past = (page_table, kv_pages, logf_pages, l * n_pool)
        ys, kvs, lfs, ss, shs = _decoder_layer(ys, state_shift[l], state_wkv[l], lw, past)
        for acc, val in zip(outs, (kvp, kvs, lfp, lfs, sp, ss, shp, shs)):
            acc.append(val)
    return (yp, ys) + tuple(jnp.stack(o) for o in outs)
```

```python
import functools

import jax
import jax.numpy as jnp
from jax import lax
from jax.experimental import pallas as pl
from jax.experimental.pallas import tpu as pltpu

F32 = jnp.float32
BF16 = jnp.bfloat16

HEAD_DIM = 64
GROUP_WIDTH = 256
N_GROUP_HEADS = 4
ATT_WIDTH = 3 * GROUP_WIDTH
N_RWKV_COLS = 1024
MOBA_BLOCK = 256
MOBA_TOPK = 3
RMS_EPS = 1e-6
RWKV_GN_EPS = 64e-5
RWKV_DECAY_OFFSET = 0.5
QK_SCALE = HEAD_DIM ** -0.5
NEG_BIG = -1e30

LANES = 128
RWKV_GROUP = 128
RWKV_CHUNK = 32
RWKV_SEQS_PER_STEP = 2
VMEM_LIMIT = 56 * 1024 * 1024


def _bf(x):
    return x.astype(BF16)


def _mm(a, b):
    return jnp.dot(_bf(a), _bf(b), preferred_element_type=F32)


def _mm_nt(a, b):
    return lax.dot_general(_bf(a), _bf(b), (((1,), (1,)), ((), ())), preferred_element_type=F32)


def _mm_tn(a, b):
    return lax.dot_general(_bf(a), _bf(b), (((0,), (0,)), ((), ())), preferred_element_type=F32)


def _split3(x):
    hi = _bf(x)
    r1 = x - hi.astype(F32)
    mid = _bf(r1)
    lo = _bf(r1 - mid.astype(F32))
    return hi, mid, lo


def _mm_exact_rhs(a, b01):
    hi, mid, lo = _split3(a)
    return (jnp.dot(hi, b01, preferred_element_type=F32) + jnp.dot(mid, b01, preferred_element_type=F32)
            + jnp.dot(lo, b01, preferred_element_type=F32))


def _mm_exact_lhs(a01, b):
    hi, mid, lo = _split3(b)
    return (jnp.dot(a01, hi, preferred_element_type=F32) + jnp.dot(a01, mid, preferred_element_type=F32)
            + jnp.dot(a01, lo, preferred_element_type=F32))


def _mm3(a, b):
    ah = _bf(a)
    al = _bf(a - ah.astype(F32))
    bh = _bf(b)
    bl = _bf(b - bh.astype(F32))
    return (jnp.dot(ah, bh, preferred_element_type=F32) + jnp.dot(ah, bl, preferred_element_type=F32)
            + jnp.dot(al, bh, preferred_element_type=F32))


def _mm3_nt(a, b):
    ah = _bf(a)
    al = _bf(a - ah.astype(F32))
    bh = _bf(b)
    bl = _bf(b - bh.astype(F32))
    dn = (((1,), (1,)), ((), ()))
    return (lax.dot_general(ah, bh, dn, preferred_element_type=F32)
            + lax.dot_general(ah, bl, dn, preferred_element_type=F32)
            + lax.dot_general(al, bh, dn, preferred_element_type=F32))


def _softplus(x):
    return jnp.maximum(x, 0.0) + jnp.log(1.0 + jnp.exp(-jnp.abs(x)))


def _log_sigmoid(x):
    return -_softplus(-x)


def _sigmoid(x):
    return 1.0 / (1.0 + jnp.exp(-x))


def _seg_ones(n, seg):
    r = lax.broadcasted_iota(jnp.int32, (n, n), 0) // seg
    c = lax.broadcasted_iota(jnp.int32, (n, n), 1) // seg
    return jnp.where(r == c, 1.0, 0.0).astype(BF16)


def _head_sum(x, seg01):
    return _mm_exact_rhs(x, seg01)


def _rms(x, g):
    return x * lax.rsqrt(jnp.mean(x * x, axis=-1, keepdims=True) + RMS_EPS) * g


def _const_spec(shape):
    nd = len(shape)
    return pl.BlockSpec(shape, lambda *_: (0,) * nd)


def _params(sem):
    return pltpu.CompilerParams(dimension_semantics=sem, vmem_limit_bytes=VMEM_LIMIT)


def _proj_kernel(*refs, prompt, tm, seq_len):
    if prompt:
        (x_ref, g_ref, wr_ref, wq_ref, wk_ref, wv_ref, wfh_ref, wfl_ref, qn_ref, kn_ref, bf_ref,
         p_ref, q_ref, kv_ref, lf_ref, c_ref, km_ref, carry_ref) = refs
    else:
        (x_ref, g_ref, wr_ref, wq_ref, wk_ref, wv_ref, wfh_ref, wfl_ref, qn_ref, kn_ref, bf_ref,
         p_ref, q_ref, kv_ref, lf_ref, c_ref) = refs
    h = _rms(x_ref[...], g_ref[...])
    hb = _bf(h)
    hl = _bf(h - hb.astype(F32))
    p_ref[...] = jnp.dot(hb, wr_ref[...], preferred_element_type=F32)
    q = jnp.dot(hb, wq_ref[...], preferred_element_type=F32)
    k = jnp.dot(hb, wk_ref[...], preferred_element_type=F32)
    v = jnp.dot(hb, wv_ref[...], preferred_element_type=F32)
    nw = 2 * GROUP_WIDTH
    seg01 = _seg_ones(nw, HEAD_DIM)

    def head_norm(z, gn):
        ms = _head_sum(z * z, seg01) * (1.0 / HEAD_DIM)
        return z * lax.rsqrt(ms + RMS_EPS) * gn

    kn = head_norm(k[:, :nw], kn_ref[...])
    q_ref[:, :nw] = head_norm(q[:, :nw], qn_ref[...])
    q_ref[:, nw:] = q[:, nw:]
    kv_ref[:, :nw] = kn
    kv_ref[:, nw:ATT_WIDTH] = k[:, nw:]
    kv_ref[:, ATT_WIDTH:] = v
    f = (jnp.dot(hb, wfh_ref[...], preferred_element_type=F32) + jnp.dot(hb, wfl_ref[...], preferred_element_type=F32)
         + jnp.dot(hl, wfh_ref[...], preferred_element_type=F32))
    lf = _log_sigmoid(f + bf_ref[...])
    lf_ref[...] = lf[:, :N_GROUP_HEADS]
    r = lax.broadcasted_iota(jnp.int32, (tm, tm), 0)
    c = lax.broadcasted_iota(jnp.int32, (tm, tm), 1)
    if prompt:
        @pl.when(pl.program_id(0) % (seq_len // tm) == 0)
        def _():
            carry_ref[...] = jnp.zeros_like(carry_ref)

        tri01 = jnp.where(c <= r, 1.0, 0.0).astype(BF16)
        cs = _mm_exact_lhs(tri01, lf) + carry_ref[...]
        carry_ref[...] = cs[tm - 1:tm, :]
        km = kn[:, GROUP_WIDTH:nw].reshape(tm // MOBA_BLOCK, MOBA_BLOCK, GROUP_WIDTH)
        km_ref[0] = jnp.sum(km, axis=1) * (1.0 / MOBA_BLOCK)
    else:
        tri01 = jnp.where((c <= r) & (r // seq_len == c // seq_len), 1.0, 0.0).astype(BF16)
        cs = _mm_exact_lhs(tri01, lf)
    c_ref[...] = cs[:, :N_GROUP_HEADS]


def _proj(x, lw, *, prompt, seq_len):
    n, d = x.shape
    tm = 512 if prompt else n
    grid = n // tm
    row = lambda w: pl.BlockSpec((tm, w), lambda i: (i, 0))
    in_specs = [row(d), _const_spec((1, d)), _const_spec((d, N_RWKV_COLS)), _const_spec((d, ATT_WIDTH)),
                _const_spec((d, ATT_WIDTH)), _const_spec((d, ATT_WIDTH)), _const_spec((d, LANES)),
                _const_spec((d, LANES)), _const_spec((1, 2 * GROUP_WIDTH)), _const_spec((1, 2 * GROUP_WIDTH)),
                _const_spec((1, LANES))]
    out_shape = [jax.ShapeDtypeStruct((n, N_RWKV_COLS), F32), jax.ShapeDtypeStruct((n, ATT_WIDTH), F32),
                 jax.ShapeDtypeStruct((n, 2 * ATT_WIDTH), F32), jax.ShapeDtypeStruct((n, N_GROUP_HEADS), F32),
                 jax.ShapeDtypeStruct((n, N_GROUP_HEADS), F32)]
    out_specs = [row(N_RWKV_COLS), row(ATT_WIDTH), row(2 * ATT_WIDTH), row(N_GROUP_HEADS), row(N_GROUP_HEADS)]
    scratch = []
    if prompt:
        nb = tm // MOBA_BLOCK
        out_shape += [jax.ShapeDtypeStruct((grid, nb, GROUP_WIDTH), F32)]
        out_specs += [pl.BlockSpec((1, nb, GROUP_WIDTH), lambda i: (i, 0, 0))]
        scratch = [pltpu.VMEM((1, LANES), F32)]
    return pl.pallas_call(
        functools.partial(_proj_kernel, prompt=prompt, tm=tm, seq_len=seq_len),
        grid=(grid,), in_specs=in_specs, out_specs=out_specs, out_shape=out_shape, scratch_shapes=scratch,
        compiler_params=_params(("arbitrary",)), name="proj_prompt" if prompt else "proj_sample",
    )(x, lw["attn_norm"], lw["w_r"], lw["w_q"], lw["w_k"], lw["w_v"], lw["w_f_hi"], lw["w_f_lo"],
      lw["q_norm"], lw["k_norm"], lw["b_forget"])


MLP_FF_CHUNK = 1024


def _mlp_kernel(*refs):
    x_ref, y_refs = refs[0], refs[1:-7]
    wo_ref, g_ref, wu_ref, wd_ref, o_ref, acc_scr, h_scr = refs[-7:]
    c = pl.program_id(1)

    @pl.when(c == 0)
    def _():
        x = x_ref[...]
        off = 0
        for y_ref in y_refs:
            w = y_ref.shape[1]
            x = x + jnp.dot(_bf(y_ref[...]), wo_ref[off:off + w, :], preferred_element_type=F32)
            off += w
        acc_scr[...] = x
        h_scr[...] = _bf(_rms(x, g_ref[...]))

    u = jnp.dot(h_scr[...], wu_ref[...], preferred_element_type=F32)
    u = jnp.square(jnp.maximum(u, 0.0))
    acc_scr[...] += jnp.dot(_bf(u), wd_ref[...], preferred_element_type=F32)

    @pl.when(c == pl.num_programs(1) - 1)
    def _():
        o_ref[...] = acc_scr[...]


def _mlp(x, ys, lw):
    n, d = x.shape
    d_ff = lw["w_up"].shape[1]
    tm = min(512, n)
    fc = MLP_FF_CHUNK
    row = lambda w: pl.BlockSpec((tm, w), lambda i, c: (i, 0))
    return pl.pallas_call(
        _mlp_kernel, grid=(n // tm, d_ff // fc),
        in_specs=([row(d)] + [row(y.shape[1]) for y in ys]
                  + [_const_spec((d, d)), _const_spec((1, d)),
                     pl.BlockSpec((d, fc), lambda i, c: (0, c)), pl.BlockSpec((fc, d), lambda i, c: (c, 0))]),
        out_specs=row(d), out_shape=jax.ShapeDtypeStruct((n, d), F32),
        scratch_shapes=[pltpu.VMEM((tm, d), F32), pltpu.VMEM((tm, d), BF16)],
        compiler_params=_params(("arbitrary", "arbitrary")), name="mlp",
    )(x, *ys, lw["w_out"], lw["mlp_norm"], lw["w_up"], lw["w_down"])


def _rwkv_kernel(p_ref, mu_ref, w0_ref, a0_ref, wwa_ref, gup_ref, kk_ref, ka_ref, rk_ref,
                 gnw_ref, gnb_ref, y_ref, s_scr, carry_scr, *, t_valid, chunk):
    t = pl.program_id(1)
    gsz = p_ref.shape[1]
    gw = GROUP_WIDTH
    p = p_ref[0]
    rows = lax.broadcasted_iota(jnp.int32, (gsz, 1), 0)
    prev = jnp.where(rows == 0, carry_scr[...], pltpu.roll(p, 1, axis=0))
    carry_scr[...] = p[gsz - 1:gsz, :]
    xm = p + (prev - p) * mu_ref[...]
    r, k, v = xm[:, :gw], xm[:, gw:2 * gw], xm[:, 2 * gw:3 * gw]
    xwa = xm[:, 3 * gw:3 * gw + LANES]
    xg = xm[:, 3 * gw + LANES:]
    lane_l = lax.broadcasted_iota(jnp.int32, (1, LANES), 1)
    lora = _mm3(jnp.where(lane_l < LANES // 2, jnp.tanh(xwa), xwa), wwa_ref[...])
    w_log = -_softplus(-(w0_ref[...] + lora[:, :gw])) - RWKV_DECAY_OFFSET
    logw = -jnp.exp(w_log)
    a = _sigmoid(a0_ref[...] + lora[:, gw:])
    g = _mm3(_sigmoid(xg), gup_ref[...])
    seg01 = _seg_ones(gw, HEAD_DIM)
    kk = k * kk_ref[...]
    kk = kk * lax.rsqrt(jnp.maximum(_head_sum(kk * kk, seg01), 1e-24))
    k2 = k * (1.0 + (a - 1.0) * ka_ref[...])
    bonus = _head_sum(r * k2 * rk_ref[...], seg01) * v
    if t_valid is not None:
        ok = (rows + t * gsz) < t_valid
        logw, kk, k2 = jnp.where(ok, logw, 0.0), jnp.where(ok, kk, 0.0), jnp.where(ok, k2, 0.0)
        r, v = jnp.where(ok, r, 0.0), jnp.where(ok, v, 0.0)
    b = kk * a

    ri = lax.broadcasted_iota(jnp.int32, (gsz, gsz), 0)
    ci = lax.broadcasted_iota(jnp.int32, (gsz, gsz), 1)
    same = (ri // chunk) == (ci // chunk)
    strict = same & (ci < ri)
    incl = same & (ci <= ri)
    lc = _mm_exact_lhs(jnp.where(incl, 1.0, 0.0).astype(BF16), logw)
    lcc = _mm_exact_lhs(jnp.where(same, 1.0, 0.0).astype(BF16), logw)
    e_neg = jnp.exp(-lc)
    e_end = jnp.exp(lcc - lc)
    at = -kk * jnp.exp(lc - logw)
    bt, kt = b * e_neg, k2 * e_neg
    rt = r * jnp.exp(lc)
    bd, kd = b * e_end, k2 * e_end
    g_end = jnp.exp(lcc)

    lane = lax.broadcasted_iota(jnp.int32, (1, gw), 1) // HEAD_DIM
    eye = jnp.where(ri == ci, 1.0, 0.0)
    rhs = jnp.concatenate([bt, kt], axis=0)
    n_sq = chunk.bit_length() - 2
    heads = range(N_GROUP_HEADS)
    masks = [lane == h for h in heads]
    grams = [_mm_nt(jnp.concatenate([jnp.where(mh, at, 0.0), jnp.where(mh, rt, 0.0)], axis=0), rhs)
             for mh in masks]
    ab = [jnp.where(strict, gm[:gsz, :gsz], 0.0) for gm in grams]
    ak = [jnp.where(strict, gm[:gsz, gsz:], 0.0) for gm in grams]
    rbs = [jnp.where(incl, gm[gsz:, :gsz], 0.0) for gm in grams]
    rk = [jnp.where(incl, gm[gsz:, gsz:], 0.0) for gm in grams]
    akv = [_mm(jnp.concatenate([ak[h], rk[h]], axis=0), v) for h in heads]
    tinv = [eye + x for x in ab]
    pw = ab
    for _ in range(n_sq):
        pw = [_mm(x, x) for x in pw]
        tinv = [tinv[h] + _mm(tinv[h], pw[h]) for h in heads]
    tw = [_mm(tinv[h], jnp.concatenate([at, akv[h][:gsz]], axis=1)) for h in heads]
    wm = _merge_lanes([x[:, :gw] for x in tw], masks)
    u0 = _merge_lanes([x[:, gw:] for x in tw], masks)
    y0 = _merge_lanes([x[gsz:] for x in akv], masks)
    yield

    r2 = lax.broadcasted_iota(jnp.int32, (gw, gw), 0) // HEAD_DIM
    c2 = lax.broadcasted_iota(jnp.int32, (gw, gw), 1) // HEAD_DIM
    blockdiag = r2 == c2
    s = s_scr[...]
    us, zs = [], []
    for c in range(gsz // chunk):
        sl = slice(c * chunk, (c + 1) * chunk)
        z = _mm_nt(jnp.concatenate([wm[sl], rt[sl]], axis=0), s)
        u = z[:chunk] + u0[sl]
        us.append(u)
        zs.append(z[chunk:])
        upd = _mm_tn(jnp.concatenate([u, v[sl]], axis=0), jnp.concatenate([bd[sl], kd[sl]], axis=0))
        s = jnp.where(blockdiag, s * g_end[c * chunk:c * chunk + 1, :] + upd, 0.0)
        yield
    s_scr[...] = s
    u_all = jnp.concatenate(us, axis=0)
    y = jnp.concatenate(zs, axis=0) + y0 + _merge_lanes([_mm(rbs[h], u_all) for h in heads], masks)

    mean = _head_sum(y, seg01) * (1.0 / HEAD_DIM)
    yc = y - mean
    var = _head_sum(yc * yc, seg01) * (1.0 / HEAD_DIM)
    yn = yc * lax.rsqrt(var + RWKV_GN_EPS) * gnw_ref[...] + gnb_ref[...]
    y_ref[0] = (yn + bonus) * g
    yield s


def _merge_lanes(parts, masks):
    out = parts[-1]
    for x, m in zip(parts[-2::-1], masks[-2::-1]):
        out = jnp.where(m, x, out)
    return out


def _rwkv_multi_kernel(p_ref, sh0_ref, s0_ref, *refs, t_valid, chunk):
    weights, (y_ref, sfin_ref, s_scr, carry_scr) = refs[:-4], refs[-4:]
    t = pl.program_id(1)

    @pl.when(t == 0)
    def _():
        carry_scr[...] = sh0_ref[...]
        s_scr[...] = s0_ref[...]

    gens = [_rwkv_kernel(p_ref.at[pl.ds(i, 1)], *weights, y_ref.at[pl.ds(i, 1)], s_scr.at[i], carry_scr.at[i],
                         t_valid=t_valid, chunk=chunk) for i in range(p_ref.shape[0])]
    states = [None] * len(gens)
    for _ in range(p_ref.shape[1] // chunk + 2):
        for i, gen in enumerate(gens):
            states[i] = next(gen)

    @pl.when(t == pl.num_programs(1) - 1)
    def _():
        for i, s in enumerate(states):
            sfin_ref[i] = s


def _blockdiag_state(s):
    b = s.shape[0]
    eye = jnp.eye(N_GROUP_HEADS, dtype=s.dtype)
    return jnp.einsum("bhij,hg->bhigj", s, eye).reshape(b, GROUP_WIDTH, GROUP_WIDTH)


def _rwkv(p, shift0, s0, lw):
    bsz, t, _ = p.shape
    gsz = RWKV_GROUP
    t_pad = -(-t // gsz) * gsz
    if t_pad != t:
        p = jnp.pad(p, ((0, 0), (0, t_pad - t), (0, 0)))
    gw = GROUP_WIDTH
    vec = lambda w: _const_spec((1, w))
    nb = RWKV_SEQS_PER_STEP if bsz % RWKV_SEQS_PER_STEP == 0 else 1
    y, s_bd = pl.pallas_call(
        functools.partial(_rwkv_multi_kernel, t_valid=None if t_pad == t else t, chunk=RWKV_CHUNK),
        grid=(bsz // nb, t_pad // gsz),
        in_specs=[pl.BlockSpec((nb, gsz, N_RWKV_COLS), lambda b, i: (b, i, 0)),
                  pl.BlockSpec((nb, 1, N_RWKV_COLS), lambda b, i: (b, 0, 0)),
                  pl.BlockSpec((nb, gw, gw), lambda b, i: (b, 0, 0)),
                  vec(N_RWKV_COLS), vec(gw), vec(gw), _const_spec((LANES, 2 * gw)), _const_spec((LANES, gw)),
                  vec(gw), vec(gw), vec(gw), vec(gw), vec(gw)],
        out_specs=[pl.BlockSpec((nb, gsz, gw), lambda b, i: (b, i, 0)),
                   pl.BlockSpec((nb, gw, gw), lambda b, i: (b, 0, 0))],
        out_shape=[jax.ShapeDtypeStruct((bsz, t_pad, gw), F32), jax.ShapeDtypeStruct((bsz, gw, gw), F32)],
        scratch_shapes=[pltpu.VMEM((nb, gw, gw), F32), pltpu.VMEM((nb, 1, N_RWKV_COLS), F32)],
        compiler_params=_params(("arbitrary", "arbitrary")), name="rwkv",
    )(p, shift0.reshape(bsz, 1, N_RWKV_COLS), _blockdiag_state(s0), lw["mu"], lw["w0"], lw["a0"], lw["w_wa"],
      lw["g_up"], lw["k_k"], lw["k_a"], lw["r_k"], lw["gn_w"], lw["gn_b"])
    s_fin = s_bd.reshape(bsz, N_GROUP_HEADS, HEAD_DIM, N_GROUP_HEADS, HEAD_DIM)
    s_fin = jnp.stack([s_fin[:, h, :, h, :] for h in range(N_GROUP_HEADS)], axis=1)
    return y[:, :t], s_fin


ATT_TILE = MOBA_BLOCK


def _head_lane(width=GROUP_WIDTH):
    return lax.broadcasted_iota(jnp.int32, (1, width), 1) // HEAD_DIM


LOG2E = 1.4426950408889634
SB_DEAD = -104.0


def _lane_bcast(col, width=LANES):
    return jnp.broadcast_to(col, (col.shape[0], width))


def _twice(x):
    return jnp.concatenate([x, x], axis=1)


def _merge_heads(parts, low):
    left = jnp.where(low, parts[0][:, :LANES], parts[1][:, :LANES])
    right = jnp.where(low, parts[2][:, LANES:], parts[3][:, LANES:])
    return jnp.concatenate([left, right], axis=1)


def _tri_tables(nq):
    pairs = [(i, j) for i in range(nq) for j in range(i + 1)]
    return (jnp.asarray([p[0] for p in pairs], jnp.int32), jnp.asarray([p[1] for p in pairs], jnp.int32))


def _flash2_kernel(qt_ref, kt_ref, *refs, mode, n_blocks):
    if mode == "fox":
        q_ref, k_ref, v_ref, cq_ref, ck_ref, o_ref, qs_scr, m_scr, l_scr, acc_scr, cq_scr = refs
    else:
        q_ref, k_ref, v_ref, km_ref, o_ref, qs_scr, m_scr, l_scr, acc_scr, sel_scr = refs
    qi = qt_ref[pl.program_id(1)]
    ki = kt_ref[pl.program_id(1)]
    tq, tk = q_ref.shape[0], k_ref.shape[0]
    nh = N_GROUP_HEADS
    lane = _head_lane()
    low = lax.broadcasted_iota(jnp.int32, (1, LANES), 1) < HEAD_DIM

    @pl.when(ki == 0)
    def _():
        m_scr[...] = jnp.full(m_scr.shape, NEG_BIG, F32)
        l_scr[...] = jnp.zeros_like(l_scr)
        acc_scr[...] = jnp.zeros_like(acc_scr)
        q = q_ref[...]
        qs = q * (QK_SCALE * LOG2E)
        for h in range(nh):
            qs_scr[h * tq:(h + 1) * tq, :] = _bf(jnp.where(lane == h, qs, 0.0))
        if mode == "fox":
            cq = cq_ref[...] * LOG2E
            for h in range(nh):
                cq_scr[h * tq:(h + 1) * tq, :] = _lane_bcast(cq[:, h:h + 1])
        else:
            km = km_ref[0]
            nb8 = -(-n_blocks // 8) * 8
            blk = lax.broadcasted_iota(jnp.int32, (nb8, 1), 0)
            for h in range(nh):
                gate = _mm3_nt(km, jnp.where(lane == h, q, 0.0))[:nb8]
                rank = jnp.zeros((nb8, tq), F32)
                for m in range(n_blocks):
                    gm = gate[m:m + 1, :]
                    tie = jnp.where(m < blk, 1.0, 0.0)
                    beats = jnp.where(gm > gate, 1.0, jnp.where(gm == gate, tie, 0.0))
                    rank = rank + beats * jnp.where(m < qi, 1.0, 0.0)
                sel = jnp.where(blk < qi, jnp.where(rank < MOBA_TOPK, 1.0, 0.0), 0.0)
                sel = jnp.concatenate([sel, jnp.zeros((LANES - nb8, tq), F32)], axis=0)
                sel_scr[h] = sel.T

    def step(diag):
        k = _bf(k_ref[...])
        v = _bf(v_ref[...])
        s_all = lax.dot_general(qs_scr[...], k, (((1,), (1,)), ((), ())), preferred_element_type=F32)
        if diag:
            causal = (lax.broadcasted_iota(jnp.int32, (1, tk), 1) <= lax.broadcasted_iota(jnp.int32, (tq, 1), 0))
        probs, alphas = [], []
        for h in range(nh):
            rows = slice(h * tq, (h + 1) * tq)
            s = s_all[rows]
            if mode == "fox":
                s = s - ck_ref[0, h:h + 1, :] * LOG2E
            if diag:
                s = jnp.where(causal, s, NEG_BIG)
            m_old = m_scr[rows]
            rmax = _lane_bcast(jnp.max(s, axis=1, keepdims=True))
            if mode == "fox":
                cq = cq_scr[rows]
                m_new = jnp.maximum(m_old, rmax + cq)
                shift = m_new - cq
            elif diag:
                m_new = jnp.maximum(m_old, rmax)
                shift = m_new
            else:
                col = lax.broadcasted_iota(jnp.int32, (1, LANES), 1)
                picked = _lane_bcast(jnp.sum(jnp.where(col == ki, sel_scr[h], 0.0), axis=1, keepdims=True)) > 0.5
                m_new = jnp.where(picked, jnp.maximum(m_old, rmax), m_old)
                shift = jnp.where(picked, m_new, -NEG_BIG)
            pr = jnp.exp2(s - _twice(shift))
            alpha = jnp.exp2(m_old - m_new)
            l_scr[rows] = alpha * l_scr[rows] + _lane_bcast(jnp.sum(pr, axis=1, keepdims=True))
            m_scr[rows] = m_new
            probs.append(_bf(pr))
            alphas.append(_twice(alpha))
        pv = jnp.dot(jnp.concatenate(probs, axis=0), v, preferred_element_type=F32)
        acc_scr[...] = (acc_scr[...] * _merge_heads(alphas, low)
                        + _merge_heads([pv[h * tq:(h + 1) * tq] for h in range(nh)], low))

    @pl.when(ki < qi)
    def _():
        step(False)

    @pl.when(ki == qi)
    def _():
        step(True)
        inv = _merge_heads([_twice(1.0 / l_scr[h * tq:(h + 1) * tq]) for h in range(nh)], low)
        o_ref[...] = acc_scr[...] * inv


def _flash2(q, kv, *, mode, batch, seq_len, q_col, c=None, c_t=None, kmean=None):
    n = q.shape[0]
    t = ATT_TILE
    nq = seq_len // t
    gw = GROUP_WIDTH
    nh = N_GROUP_HEADS
    v_col = q_col + ATT_WIDTH // gw
    in_specs = [pl.BlockSpec((t, gw), lambda b, s, qt, kt: (b * nq + qt[s], q_col)),
                pl.BlockSpec((t, gw), lambda b, s, qt, kt: (b * nq + kt[s], q_col)),
                pl.BlockSpec((t, gw), lambda b, s, qt, kt: (b * nq + kt[s], v_col))]
    scratch = [pltpu.VMEM((nh * t, gw), BF16), pltpu.VMEM((nh * t, LANES), F32), pltpu.VMEM((nh * t, LANES), F32),
               pltpu.VMEM((t, gw), F32)]
    if mode == "fox":
        in_specs += [pl.BlockSpec((t, nh), lambda b, s, qt, kt: (b * nq + qt[s], 0)),
                     pl.BlockSpec((1, nh, t), lambda b, s, qt, kt: (b, 0, kt[s]))]
        scratch += [pltpu.VMEM((nh * t, LANES), F32)]
        args = (q, kv, kv, c, c_t)
    else:
        in_specs += [pl.BlockSpec((1, LANES, gw), lambda b, s, qt, kt: (b, 0, 0))]
        scratch += [pltpu.VMEM((nh, t, LANES), F32)]
        args = (q, kv, kv, kmean)
    qt, kt = _tri_tables(nq)
    return pl.pallas_call(
        functools.partial(_flash2_kernel, mode=mode, n_blocks=nq),
        grid_spec=pltpu.PrefetchScalarGridSpec(
            num_scalar_prefetch=2, grid=(batch, qt.shape[0]), in_specs=in_specs,
            out_specs=pl.BlockSpec((t, gw), lambda b, s, qt, kt: (b * nq + qt[s], 0)), scratch_shapes=scratch),
        out_shape=jax.ShapeDtypeStruct((n, gw), F32),
        compiler_params=_params(("arbitrary", "arbitrary")), name="attn_" + mode,
    )(qt, kt, *args)


def _sb2_kernel(qt_ref, st_ref, q_ref, k_ref, v_ref, o_ref, qs_scr, r_scr, acc_scr, dead_ref):
    qi = qt_ref[pl.program_id(1)]
    step_id = st_ref[pl.program_id(1)]
    tq, tk = q_ref.shape[0], k_ref.shape[0]
    nh = N_GROUP_HEADS
    lane = _head_lane()
    low = lax.broadcasted_iota(jnp.int32, (1, LANES), 1) < HEAD_DIM

    @pl.when(step_id == 0)
    def _():
        r_scr[...] = jnp.zeros_like(r_scr)
        acc_scr[...] = jnp.zeros_like(acc_scr)
        dead_ref[0] = 0
        qs = q_ref[...] * QK_SCALE
        for h in range(nh):
            qs_scr[h * tq:(h + 1) * tq, :] = _bf(jnp.where(lane == h, qs, 0.0))

    def step(diag):
        k = _bf(k_ref[...])
        v = _bf(v_ref[...])
        z = lax.dot_general(qs_scr[...], k, (((1,), (1,)), ((), ())), preferred_element_type=F32)
        soft = jnp.maximum(z, 0.0) + jnp.log(1.0 + jnp.exp(-jnp.abs(z)))
        log_beta = z - soft
        keep = -soft
        if diag:
            qrow = lax.broadcasted_iota(jnp.int32, (nh * tq, 1), 0) % tq
            strict = lax.broadcasted_iota(jnp.int32, (1, tk), 1) < qrow
            keep = jnp.where(strict, keep, 0.0)
        later01 = jnp.where(lax.broadcasted_iota(jnp.int32, (tk, tk), 0) > lax.broadcasted_iota(jnp.int32, (tk, tk), 1),
                            1.0, 0.0).astype(BF16)
        r_old = r_scr[...]
        att = jnp.exp(log_beta + _mm_exact_rhs(keep, later01) + _twice(r_old))
        if diag:
            att = jnp.where(strict, att, 0.0)
        pv = jnp.dot(_bf(att), v, preferred_element_type=F32)
        acc_scr[...] += _merge_heads([pv[h * tq:(h + 1) * tq] for h in range(nh)], low)
        r_new = r_old + _lane_bcast(jnp.sum(keep, axis=1, keepdims=True))
        r_scr[...] = r_new
        dead_ref[0] = (jnp.max(r_new) < SB_DEAD).astype(jnp.int32)

    @pl.when(step_id == 0)
    def _():
        step(True)

    @pl.when((step_id > 0) & (dead_ref[0] == 0))
    def _():
        step(False)

    @pl.when(step_id == qi)
    def _():
        o_ref[...] = acc_scr[...]


def _sb2(q, kv, *, batch, seq_len, q_col):
    n = q.shape[0]
    t = ATT_TILE
    nq = seq_len // t
    gw = GROUP_WIDTH
    nh = N_GROUP_HEADS
    v_col = q_col + ATT_WIDTH // gw
    kblk = lambda b, s, qt, st: b * nq + qt[s] - st[s]
    qt, st = _tri_tables(nq)
    return pl.pallas_call(
        _sb2_kernel,
        grid_spec=pltpu.PrefetchScalarGridSpec(
            num_scalar_prefetch=2, grid=(batch, qt.shape[0]),
            in_specs=[pl.BlockSpec((t, gw), lambda b, s, qt, st: (b * nq + qt[s], q_col)),
                      pl.BlockSpec((t, gw), lambda b, s, qt, st: (kblk(b, s, qt, st), q_col)),
                      pl.BlockSpec((t, gw), lambda b, s, qt, st: (kblk(b, s, qt, st), v_col))],
            out_specs=pl.BlockSpec((t, gw), lambda b, s, qt, st: (b * nq + qt[s], 0)),
            scratch_shapes=[pltpu.VMEM((nh * t, gw), BF16), pltpu.VMEM((nh * t, LANES), F32),
                            pltpu.VMEM((t, gw), F32), pltpu.SMEM((1,), jnp.int32)]),
        out_shape=jax.ShapeDtypeStruct((n, gw), F32),
        compiler_params=_params(("arbitrary", "arbitrary")), name="attn_sb",
    )(qt, st, q, kv, kv)


PAGE = 128
LOGF_PAGES_PER_STEP = 16


def _foxpast_kernel(pt_ref, *refs):
    n_grp = LOGF_PAGES_PER_STEP
    page_refs, o_ref, x_scr, carry_scr = refs[:n_grp], refs[n_grp], refs[n_grp + 1], refs[n_grp + 2]
    del pt_ref

    @pl.when(pl.program_id(1) == 0)
    def _():
        carry_scr[...] = jnp.zeros_like(carry_scr)

    for i in range(n_grp):
        x_scr[i:i + 1, :] = page_refs[i][0]
    x = x_scr[...]
    nh = N_GROUP_HEADS
    li = lax.broadcasted_iota(jnp.int32, (PAGE * nh, PAGE), 0)
    ki = lax.broadcasted_iota(jnp.int32, (PAGE * nh, PAGE), 1)
    later01 = jnp.where(lax.broadcasted_iota(jnp.int32, (PAGE, PAGE), 0) > lax.broadcasted_iota(jnp.int32, (PAGE, PAGE), 1),
                        1.0, 0.0).astype(BF16)
    later_pg01 = jnp.where(lax.broadcasted_iota(jnp.int32, (n_grp, n_grp), 1) > lax.broadcasted_iota(jnp.int32, (n_grp, n_grp), 0),
                           1.0, 0.0).astype(BF16)
    for h in range(nh):
        pick01 = jnp.where((li // nh == ki) & (li % nh == h), 1.0, 0.0).astype(BF16)
        lh = _mm_exact_rhs(x, pick01)
        within = _mm_exact_rhs(lh, later01)
        later_pages = jnp.sum(_mm_exact_lhs(later_pg01, lh), axis=1, keepdims=True)
        total = jnp.sum(jnp.sum(lh, axis=1, keepdims=True), axis=0, keepdims=True)
        carry = carry_scr[h:h + 1, :]
        o_ref[0, h] = within + later_pages + carry
        carry_scr[h:h + 1, :] = carry + total


def _fox_past(page_table, logf_pages, page_offset):
    bsz, n_pages = page_table.shape
    n_grp = LOGF_PAGES_PER_STEP
    groups = n_pages // n_grp

    def page_spec(i):
        return pl.BlockSpec((1, 1, PAGE * N_GROUP_HEADS),
                            lambda b, g, pt: (page_offset + pt[b * n_pages + (groups - 1 - g) * n_grp + i], 0, 0))

    out = pl.pallas_call(
        _foxpast_kernel,
        grid_spec=pltpu.PrefetchScalarGridSpec(
            num_scalar_prefetch=1, grid=(bsz, groups),
            in_specs=[page_spec(i) for i in range(n_grp)],
            out_specs=pl.BlockSpec((1, N_GROUP_HEADS, n_grp, PAGE), lambda b, g, pt: (b, 0, groups - 1 - g, 0)),
            scratch_shapes=[pltpu.VMEM((n_grp, PAGE * N_GROUP_HEADS), F32), pltpu.VMEM((8, PAGE), F32)]),
        out_shape=jax.ShapeDtypeStruct((bsz, N_GROUP_HEADS, n_pages, PAGE), F32),
        compiler_params=_params(("arbitrary", "arbitrary")), name="fox_past",
    )(page_table.reshape(-1), *([logf_pages] * n_grp))
    return out.reshape(bsz, N_GROUP_HEADS, n_pages * PAGE)


N_MIXERS = ATT_WIDTH // GROUP_WIDTH


DECODE_BLOCKS_PER_STEP = 4


def _decode3_kernel(pt_ref, q_ref, kvn_ref, cn_ref, cnt_ref, sfx_ref, *refs, n_blocks):
    del pt_ref
    bps = DECODE_BLOCKS_PER_STEP
    page_refs, o_ref = refs[:2 * bps], refs[2 * bps]
    (qbd_scr, cn_scr, fm_scr, fl_scr, facc_scr, sr_scr, sacc_scr, mg_scr, mm_scr, ml_scr, mo_scr,
     om_scr, ol_scr, oo_scr) = refs[2 * bps + 1:]
    s_id = pl.program_id(1)
    nq = q_ref.shape[0]
    nh = N_GROUP_HEADS
    gw = GROUP_WIDTH
    rows = nh * nq
    lane = lax.broadcasted_iota(jnp.int32, (1, LANES), 1)
    row_head = lax.broadcasted_iota(jnp.int32, (rows, 1), 0) // nq
    qrow = lax.broadcasted_iota(jnp.int32, (rows, 1), 0) % nq
    own_lanes = row_head == _head_lane()

    @pl.when(s_id == 0)
    def _():
        for m in range(N_MIXERS):
            qm = jnp.concatenate([q_ref[:, m * gw:(m + 1) * gw]] * nh, axis=0)
            qbd_scr[m] = jnp.where(own_lanes, qm, 0.0)
        cn = cn_ref[...]
        cn_scr[...] = jnp.concatenate([_lane_bcast(cn[:, h:h + 1]) for h in range(nh)], axis=0)
        fm_scr[...] = jnp.full(fm_scr.shape, NEG_BIG, F32)
        for ref in (fl_scr, facc_scr, sr_scr, sacc_scr, mg_scr, mm_scr, ml_scr):
            ref[...] = jnp.zeros_like(ref)

    def tile_pieces(k_tile, v_tile, fox_rows, mask_incl, mask_strict):
        nk = k_tile.shape[0]
        kb = _bf(k_tile)
        vb = _bf(v_tile)
        dn = (((1,), (1,)), ((), ()))
        z = [lax.dot_general(_bf(qbd_scr[m] * QK_SCALE), kb[:, m * gw:(m + 1) * gw], dn, preferred_element_type=F32)
             for m in range(N_MIXERS)]
        yield
        bias = jnp.concatenate([jnp.broadcast_to(fox_rows[h:h + 1, :], (nq, nk)) for h in range(nh)], axis=0)
        f_sc = z[0] + bias
        m_sc = z[1]
        soft = jnp.maximum(z[2], 0.0) + jnp.log(1.0 + jnp.exp(-jnp.abs(z[2])))
        keep = -soft
        if mask_incl is not None:
            f_sc = jnp.where(mask_incl, f_sc, NEG_BIG)
            m_sc = jnp.where(mask_incl, m_sc, NEG_BIG)
            keep = jnp.where(mask_strict, keep, 0.0)
        f_m = _lane_bcast(jnp.max(f_sc, axis=1, keepdims=True))
        m_m = _lane_bcast(jnp.max(m_sc, axis=1, keepdims=True))
        later01 = jnp.where(lax.broadcasted_iota(jnp.int32, (nk, nk), 0) > lax.broadcasted_iota(jnp.int32, (nk, nk), 1),
                            1.0, 0.0).astype(BF16)
        right = _mm_exact_rhs(keep, later01)
        yield
        f_pr = jnp.exp(f_sc - _lane_bcast(f_m[:, :1], nk))
        m_pr = jnp.exp(m_sc - _lane_bcast(m_m[:, :1], nk))
        att = jnp.exp(z[2] - soft + right)
        if mask_incl is not None:
            att = jnp.where(mask_strict, att, 0.0)
        yield
        out = dict(
            f_m=f_m, f_l=_lane_bcast(jnp.sum(f_pr, axis=1, keepdims=True)),
            f_o=jnp.dot(_bf(f_pr), vb[:, :gw], preferred_element_type=F32),
            m_m=m_m, m_l=_lane_bcast(jnp.sum(m_pr, axis=1, keepdims=True)),
            m_o=jnp.dot(_bf(m_pr), vb[:, gw:2 * gw], preferred_element_type=F32),
            s_keep=_lane_bcast(jnp.sum(keep, axis=1, keepdims=True)),
            s_o=jnp.dot(_bf(att), vb[:, 2 * gw:], preferred_element_type=F32),
            gate=_lane_bcast(jnp.sum(qbd_scr[1] * (jnp.sum(k_tile[:, gw:2 * gw], axis=0, keepdims=True)
                                                   * (1.0 / MOBA_BLOCK)), axis=1, keepdims=True)))
        yield out

    def run_tiles(gens):
        outs = [None] * len(gens)
        for _ in range(4):
            for i, gen in enumerate(gens):
                outs[i] = next(gen)
        return outs

    def merge(pc, block):
        cn = cn_scr[...]
        m_old = fm_scr[...]
        m_tile = pc["f_m"] + cn
        m_new = jnp.maximum(m_old, m_tile)
        a_old = jnp.exp(m_old - m_new)
        a_tile = jnp.exp(m_tile - m_new)
        fm_scr[...] = m_new
        fl_scr[...] = a_old * fl_scr[...] + a_tile * pc["f_l"]
        facc_scr[...] = _twice(a_old) * facc_scr[...] + _twice(a_tile) * pc["f_o"]
        r_old = sr_scr[...]
        sacc_scr[...] = sacc_scr[...] + _twice(jnp.exp(r_old)) * pc["s_o"]
        sr_scr[...] = r_old + pc["s_keep"]
        if block is None:
            om_scr[...], ol_scr[...], oo_scr[...] = pc["m_m"], pc["m_l"], pc["m_o"]
        else:
            here = lane == block
            mg_scr[...] = jnp.where(here, pc["gate"], mg_scr[...])
            mm_scr[...] = jnp.where(here, pc["m_m"], mm_scr[...])
            ml_scr[...] = jnp.where(here, pc["m_l"], ml_scr[...])
            mo_scr[pl.ds(block, 1)] = pc["m_o"][None]

    @pl.when(s_id == 0)
    def _():
        pad = jnp.zeros((PAGE - nq, 2 * ATT_WIDTH), F32)
        kvn = jnp.concatenate([kvn_ref[...], pad], axis=0)
        key = lax.broadcasted_iota(jnp.int32, (1, PAGE), 1)
        (pc,) = run_tiles([tile_pieces(kvn[:, :ATT_WIDTH], kvn[:, ATT_WIDTH:], -cnt_ref[0], key <= qrow, key < qrow)])
        merge(pc, None)

    first = (pl.num_programs(1) - 1 - s_id) * bps
    gens = []
    for i in range(bps - 1, -1, -1):
        p0, p1 = page_refs[2 * i], page_refs[2 * i + 1]
        k_tile = jnp.concatenate([p0[0, :, :ATT_WIDTH], p1[0, :, :ATT_WIDTH]], axis=0)
        v_tile = jnp.concatenate([p0[0, :, ATT_WIDTH:], p1[0, :, ATT_WIDTH:]], axis=0)
        gens.append(tile_pieces(k_tile, v_tile, sfx_ref[0, :, i * MOBA_BLOCK:(i + 1) * MOBA_BLOCK], None, None))
    for j, pc in enumerate(run_tiles(gens)):
        merge(pc, first + bps - 1 - j)

    @pl.when(s_id == pl.num_programs(1) - 1)
    def _():
        def store(m, res):
            out = jnp.zeros((nq, gw), F32)
            for h in range(nh):
                out = jnp.where(_head_lane() == h, res[h * nq:(h + 1) * nq], out)
            o_ref[:, m * gw:(m + 1) * gw] = out

        store(0, facc_scr[...] / _twice(fl_scr[...]))
        store(2, sacc_scr[...])
        gate, m_all, l_all = mg_scr[...], mm_scr[...], ml_scr[...]
        rank = jnp.zeros((rows, LANES), F32)
        for n in range(n_blocks):
            gn = gate[:, n:n + 1]
            tie = jnp.where(n < lane, 1.0, 0.0)
            rank = rank + jnp.where(gn > gate, 1.0, jnp.where(gn == gate, tie, 0.0))
        sel = jnp.where(lane < n_blocks, jnp.where(rank < MOBA_TOPK, 1.0, 0.0), 0.0) > 0.5
        m_own = om_scr[...]
        m_top = jnp.maximum(_lane_bcast(jnp.max(jnp.where(sel, m_all, NEG_BIG), axis=1, keepdims=True)), m_own)
        w = jnp.where(sel, jnp.exp(m_all - m_top), 0.0)
        w_own = jnp.exp(m_own - m_top)
        den = _lane_bcast(jnp.sum(w * l_all, axis=1, keepdims=True)) + w_own * ol_scr[...]
        num = _twice(w_own) * oo_scr[...]
        for n in range(n_blocks):
            num = num + _lane_bcast(w[:, n:n + 1], gw) * mo_scr[n]
        store(1, num / _twice(den))


def _decode3(page_table, q, kv_new, c_new, sfx, kv_pages, page_offset):
    bsz, n_pages = page_table.shape
    nq = q.shape[0] // bsz
    nh = N_GROUP_HEADS
    gw = GROUP_WIDTH
    bps = DECODE_BLOCKS_PER_STEP
    n_blocks = n_pages * PAGE // MOBA_BLOCK
    assert MOBA_BLOCK == 2 * PAGE and n_blocks <= LANES and n_blocks % bps == 0
    n_steps = n_blocks // bps
    cnt = jnp.pad(jnp.swapaxes(c_new.reshape(bsz, nq, nh), 1, 2), ((0, 0), (0, 0), (0, PAGE - nq)))

    def page_spec(r):
        return pl.BlockSpec((1, PAGE, 2 * ATT_WIDTH),
                            lambda b, s, pt: (page_offset + pt[b * n_pages + 2 * bps * (n_steps - 1 - s) + r], 0, 0))

    rows = nh * nq
    stat = lambda: pltpu.VMEM((rows, LANES), F32)
    wide = lambda: pltpu.VMEM((rows, gw), F32)
    return pl.pallas_call(
        functools.partial(_decode3_kernel, n_blocks=n_blocks),
        grid_spec=pltpu.PrefetchScalarGridSpec(
            num_scalar_prefetch=1, grid=(bsz, n_steps),
            in_specs=[pl.BlockSpec((nq, ATT_WIDTH), lambda b, s, pt: (b, 0)),
                      pl.BlockSpec((nq, 2 * ATT_WIDTH), lambda b, s, pt: (b, 0)),
                      pl.BlockSpec((nq, nh), lambda b, s, pt: (b, 0)),
                      pl.BlockSpec((1, nh, PAGE), lambda b, s, pt: (b, 0, 0)),
                      pl.BlockSpec((1, nh, bps * MOBA_BLOCK), lambda b, s, pt: (b, 0, n_steps - 1 - s))]
                     + [page_spec(r) for r in range(2 * bps)],
            out_specs=pl.BlockSpec((nq, ATT_WIDTH), lambda b, s, pt: (b, 0)),
            scratch_shapes=[pltpu.VMEM((N_MIXERS, rows, gw), F32), stat(), stat(), stat(), wide(), stat(), wide(),
                            stat(), stat(), stat(), pltpu.VMEM((n_blocks, rows, gw), F32), stat(), stat(), wide()]),
        out_shape=jax.ShapeDtypeStruct((bsz * nq, ATT_WIDTH), F32),
        compiler_params=_params(("arbitrary", "arbitrary")), name="decode_attn",
    )(page_table.reshape(-1), q, kv_new, c_new, cnt, sfx, *([kv_pages] * (2 * bps)))


def _layer_weights(l, attn_norm, w_in, rwkv_mu, rwkv_w0, rwkv_w_up, rwkv_a0, rwkv_a_up, rwkv_g_up,
                   rwkv_k_k, rwkv_k_a, rwkv_r_k, rwkv_gn_w, rwkv_gn_b, q_norm, k_norm, b_forget,
                   w_out, mlp_norm, w_mlp_up, w_mlp_down):
    w = w_in[l]
    a0, a1, a2, a3 = N_RWKV_COLS, N_RWKV_COLS + ATT_WIDTH, N_RWKV_COLS + 2 * ATT_WIDTH, N_RWKV_COLS + 3 * ATT_WIDTH
    w_f = jnp.pad(w[:, a3:], ((0, 0), (0, LANES - N_GROUP_HEADS)))
    w_f_hi = w_f.astype(BF16)
    rank = rwkv_w_up.shape[1]
    z = jnp.zeros((rank, GROUP_WIDTH), F32)
    w_wa = jnp.concatenate([jnp.concatenate([rwkv_w_up[l], z], axis=1),
                            jnp.concatenate([z, rwkv_a_up[l]], axis=1)], axis=0)
    row = lambda v: v.reshape(1, -1)
    return dict(
        attn_norm=row(attn_norm[l]), w_r=w[:, :a0].astype(BF16), w_q=w[:, a0:a1].astype(BF16),
        w_k=w[:, a1:a2].astype(BF16), w_v=w[:, a2:a3].astype(BF16), w_f_hi=w_f_hi,
        w_f_lo=(w_f - w_f_hi.astype(F32)).astype(BF16), q_norm=row(q_norm[l]), k_norm=row(k_norm[l]),
        b_forget=jnp.pad(row(b_forget[l]), ((0, 0), (0, LANES - N_GROUP_HEADS))),
        mu=row(rwkv_mu[l]), w0=row(rwkv_w0[l]), a0=row(rwkv_a0[l]), w_wa=w_wa, g_up=rwkv_g_up[l],
        k_k=row(rwkv_k_k[l]), k_a=row(rwkv_k_a[l]), r_k=row(rwkv_r_k[l]), gn_w=row(rwkv_gn_w[l]),
        gn_b=row(rwkv_gn_b[l]), w_out=w_out[l].astype(BF16), mlp_norm=row(mlp_norm[l]),
        w_up=w_mlp_up[l].astype(BF16), w_down=w_mlp_down[l].astype(BF16))


def _decoder_layer(x, shift0, s0, lw, past):
    bsz, t, d = x.shape
    n = bsz * t
    xf = x.reshape(n, d)
    prompt = past is None
    outs = _proj(xf, lw, prompt=prompt, seq_len=t)
    p, q, kv, lf, c = outs[:5]
    p3 = p.reshape(bsz, t, N_RWKV_COLS)
    y_rwkv, s_new = _rwkv(p3, shift0, s0, lw)
    if prompt:
        kmean = outs[5].reshape(bsz, t // MOBA_BLOCK, GROUP_WIDTH)
        kmean = jnp.pad(kmean, ((0, 0), (0, LANES - t // MOBA_BLOCK), (0, 0)))
        c_t = jnp.swapaxes(c.reshape(bsz, t, N_GROUP_HEADS), 1, 2)
        ys = [_flash2(q, kv, mode="fox", batch=bsz, seq_len=t, q_col=0, c=c, c_t=c_t),
              _flash2(q, kv, mode="moba", batch=bsz, seq_len=t, q_col=1, kmean=kmean),
              _sb2(q, kv, batch=bsz, seq_len=t, q_col=2)]
    else:
        page_table, kv_pages, logf_pages, page_offset = past
        sfx = _fox_past(page_table, logf_pages, page_offset)
        ys = [_decode3(page_table, q, kv, c, sfx, kv_pages, page_offset)]
    x_new = _mlp(xf, [y_rwkv.reshape(n, GROUP_WIDTH)] + ys, lw)
    kv_new = kv.reshape(bsz, t, 2, ATT_WIDTH // HEAD_DIM, HEAD_DIM)
    return x_new.reshape(bsz, t, d), kv_new, lf.reshape(bsz, t, N_GROUP_HEADS), s_new, p3[:, -1]


def kernel(x_prompt, x_sample, cache_kv, cache_logf, state_wkv, state_shift, page_table, attn_norm, w_in, rwkv_mu,
           rwkv_w0, rwkv_w_up, rwkv_a0, rwkv_a_up, rwkv_g_up, rwkv_k_k, rwkv_k_a, rwkv_r_k, rwkv_gn_w, rwkv_gn_b,
           q_norm, k_norm, b_forget, w_out, mlp_norm, w_mlp_up, w_mlp_down):
    depth, n_pool = cache_kv.shape[:2]
    bp = x_prompt.shape[0]
    kv_pages = cache_kv.reshape(depth * n_pool, PAGE, 2 * ATT_WIDTH)
    logf_pages = cache_logf.reshape(depth * n_pool, 1, PAGE * N_GROUP_HEADS)
    yp, ys = x_prompt, x_sample
    outs = [[] for _ in range(8)]
    for l in range(depth):
        lw = _layer_weights(l, attn_norm, w_in, rwkv_mu, rwkv_w0, rwkv_w_up, rwkv_a0, rwkv_a_up, rwkv_g_up,
                            rwkv_k_k, rwkv_k_a, rwkv_r_k, rwkv_gn_w, rwkv_gn_b, q_norm, k_norm, b_forget,
                            w_out, mlp_norm, w_mlp_up, w_mlp_down)
        shift0 = jnp.zeros((bp, N_RWKV_COLS), yp.dtype)
        s0 = jnp.zeros((bp, N_GROUP_HEADS, HEAD_DIM, HEAD_DIM), yp.dtype)
        yp, kvp, lfp, sp, shp = _decoder_layer(yp, shift0, s0, lw, None)
        past = (page_table, kv_pages, logf_pages, l * n_pool)
        ys, kvs, lfs, ss, shs = _decoder_layer(ys, state_shift[l], state_wkv[l], lw, past)
        for acc, val in zip(outs, (kvp, kvs, lfp, lfs, sp, ss, shp, shs)):
            acc.append(val)
    return (yp, ys) + tuple(jnp.stack(o) for o in outs)
```

```python
import functools

import jax
import jax.numpy as jnp
from jax import lax
from jax.experimental import pallas as pl
from jax.experimental.pallas import tpu as pltpu

F32 = jnp.float32
BF16 = jnp.bfloat16

HEAD_DIM = 64
GROUP_WIDTH = 256
N_GROUP_HEADS = 4
ATT_WIDTH = 3 * GROUP_WIDTH
N_RWKV_COLS = 1024
MOBA_BLOCK = 256
MOBA_TOPK = 3
RMS_EPS = 1e-6
RWKV_GN_EPS = 64e-5
RWKV_DECAY_OFFSET = 0.5
QK_SCALE = HEAD_DIM ** -0.5
NEG_BIG = -1e30

LANES = 128
RWKV_GROUP = 128
RWKV_CHUNK = 32
RWKV_SEQS_PER_STEP = 2
VMEM_LIMIT = 56 * 1024 * 1024


def _bf(x):
    return x.astype(BF16)


def _mm(a, b):
    return jnp.dot(_bf(a), _bf(b), preferred_element_type=F32)


def _mm_nt(a, b):
    return lax.dot_general(_bf(a), _bf(b), (((1,), (1,)), ((), ())), preferred_element_type=F32)


def _mm_tn(a, b):
    return lax.dot_general(_bf(a), _bf(b), (((0,), (0,)), ((), ())), preferred_element_type=F32)


def _split3(x):
    hi = _bf(x)
    r1 = x - hi.astype(F32)
    mid = _bf(r1)
    lo = _bf(r1 - mid.astype(F32))
    return hi, mid, lo


def _mm_exact_rhs(a, b01):
    hi, mid, lo = _split3(a)
    return (jnp.dot(hi, b01, preferred_element_type=F32) + jnp.dot(mid, b01, preferred_element_type=F32)
            + jnp.dot(lo, b01, preferred_element_type=F32))


def _mm_exact_lhs(a01, b):
    hi, mid, lo = _split3(b)
    return (jnp.dot(a01, hi, preferred_element_type=F32) + jnp.dot(a01, mid, preferred_element_type=F32)
            + jnp.dot(a01, lo, preferred_element_type=F32))


def _mm_fine_rhs(a, b01):
    hi = _bf(a)
    mid = _bf(a - hi.astype(F32))
    return jnp.dot(hi, b01, preferred_element_type=F32) + jnp.dot(mid, b01, preferred_element_type=F32)


def _mm_fine_lhs(a01, b):
    hi = _bf(b)
    mid = _bf(b - hi.astype(F32))
    return jnp.dot(a01, hi, preferred_element_type=F32) + jnp.dot(a01, mid, preferred_element_type=F32)


def _mm3(a, b):
    ah = _bf(a)
    al = _bf(a - ah.astype(F32))
    bh = _bf(b)
    bl = _bf(b - bh.astype(F32))
    return (jnp.dot(ah, bh, preferred_element_type=F32) + jnp.dot(ah, bl, preferred_element_type=F32)
            + jnp.dot(al, bh, preferred_element_type=F32))


def _mm3_nt(a, b):
    ah = _bf(a)
    al = _bf(a - ah.astype(F32))
    bh = _bf(b)
    bl = _bf(b - bh.astype(F32))
    dn = (((1,), (1,)), ((), ()))
    return (lax.dot_general(ah, bh, dn, preferred_element_type=F32)
            + lax.dot_general(ah, bl, dn, preferred_element_type=F32)
            + lax.dot_general(al, bh, dn, preferred_element_type=F32))


def _softplus(x):
    return jnp.maximum(x, 0.0) + jnp.log(1.0 + jnp.exp(-jnp.abs(x)))


def _log_sigmoid(x):
    return -_softplus(-x)


def _sigmoid(x):
    return 1.0 / (1.0 + jnp.exp(-x))


def _seg_ones(n, seg):
    r = lax.broadcasted_iota(jnp.int32, (n, n), 0) // seg
    c = lax.broadcasted_iota(jnp.int32, (n, n), 1) // seg
    return jnp.where(r == c, 1.0, 0.0).astype(BF16)


def _head_sum(x, seg01):
    return _mm_fine_rhs(x, seg01)


def _rms(x, g):
    return x * lax.rsqrt(jnp.mean(x * x, axis=-1, keepdims=True) + RMS_EPS) * g


def _const_spec(shape):
    nd = len(shape)
    return pl.BlockSpec(shape, lambda *_: (0,) * nd)


def _params(sem):
    return pltpu.CompilerParams(dimension_semantics=sem, vmem_limit_bytes=VMEM_LIMIT)


def _proj_kernel(*refs, prompt, tm, seq_len):
    if prompt:
        (x_ref, g_ref, wr_ref, wq_ref, wk_ref, wv_ref, wfh_ref, wfl_ref, qn_ref, kn_ref, bf_ref,
         p_ref, q_ref, kv_ref, lf_ref, c_ref, km_ref, carry_ref) = refs
    else:
        (x_ref, g_ref, wr_ref, wq_ref, wk_ref, wv_ref, wfh_ref, wfl_ref, qn_ref, kn_ref, bf_ref,
         p_ref, q_ref, kv_ref, lf_ref, c_ref) = refs
    h = _rms(x_ref[...], g_ref[...])
    hb = _bf(h)
    hl = _bf(h - hb.astype(F32))
    p_ref[...] = jnp.dot(hb, wr_ref[...], preferred_element_type=F32)
    q = jnp.dot(hb, wq_ref[...], preferred_element_type=F32)
    k = jnp.dot(hb, wk_ref[...], preferred_element_type=F32)
    v = jnp.dot(hb, wv_ref[...], preferred_element_type=F32)
    nw = 2 * GROUP_WIDTH
    seg01 = _seg_ones(nw, HEAD_DIM)

    def head_norm(z, gn):
        ms = _head_sum(z * z, seg01) * (1.0 / HEAD_DIM)
        return z * lax.rsqrt(ms + RMS_EPS) * gn

    kn = head_norm(k[:, :nw], kn_ref[...])
    q_ref[:, :nw] = head_norm(q[:, :nw], qn_ref[...])
    q_ref[:, nw:] = q[:, nw:]
    kv_ref[:, :nw] = kn
    kv_ref[:, nw:ATT_WIDTH] = k[:, nw:]
    kv_ref[:, ATT_WIDTH:] = v
    f = (jnp.dot(hb, wfh_ref[...], preferred_element_type=F32) + jnp.dot(hb, wfl_ref[...], preferred_element_type=F32)
         + jnp.dot(hl, wfh_ref[...], preferred_element_type=F32))
    lf = _log_sigmoid(f + bf_ref[...])
    lf_ref[...] = lf[:, :N_GROUP_HEADS]
    r = lax.broadcasted_iota(jnp.int32, (tm, tm), 0)
    c = lax.broadcasted_iota(jnp.int32, (tm, tm), 1)
    if prompt:
        @pl.when(pl.program_id(0) % (seq_len // tm) == 0)
        def _():
            carry_ref[...] = jnp.zeros_like(carry_ref)

        tri01 = jnp.where(c <= r, 1.0, 0.0).astype(BF16)
        cs = _mm_exact_lhs(tri01, lf) + carry_ref[...]
        carry_ref[...] = cs[tm - 1:tm, :]
        km = kn[:, GROUP_WIDTH:nw].reshape(tm // MOBA_BLOCK, MOBA_BLOCK, GROUP_WIDTH)
        km_ref[0] = jnp.sum(km, axis=1) * (1.0 / MOBA_BLOCK)
    else:
        tri01 = jnp.where((c <= r) & (r // seq_len == c // seq_len), 1.0, 0.0).astype(BF16)
        cs = _mm_exact_lhs(tri01, lf)
    c_ref[...] = cs[:, :N_GROUP_HEADS]


def _proj(x, lw, *, prompt, seq_len):
    n, d = x.shape
    tm = 512 if prompt else n
    grid = n // tm
    row = lambda w: pl.BlockSpec((tm, w), lambda i: (i, 0))
    in_specs = [row(d), _const_spec((1, d)), _const_spec((d, N_RWKV_COLS)), _const_spec((d, ATT_WIDTH)),
                _const_spec((d, ATT_WIDTH)), _const_spec((d, ATT_WIDTH)), _const_spec((d, LANES)),
                _const_spec((d, LANES)), _const_spec((1, 2 * GROUP_WIDTH)), _const_spec((1, 2 * GROUP_WIDTH)),
                _const_spec((1, LANES))]
    out_shape = [jax.ShapeDtypeStruct((n, N_RWKV_COLS), F32), jax.ShapeDtypeStruct((n, ATT_WIDTH), F32),
                 jax.ShapeDtypeStruct((n, 2 * ATT_WIDTH), F32), jax.ShapeDtypeStruct((n, N_GROUP_HEADS), F32),
                 jax.ShapeDtypeStruct((n, N_GROUP_HEADS), F32)]
    out_specs = [row(N_RWKV_COLS), row(ATT_WIDTH), row(2 * ATT_WIDTH), row(N_GROUP_HEADS), row(N_GROUP_HEADS)]
    scratch = []
    if prompt:
        nb = tm // MOBA_BLOCK
        out_shape += [jax.ShapeDtypeStruct((grid, nb, GROUP_WIDTH), F32)]
        out_specs += [pl.BlockSpec((1, nb, GROUP_WIDTH), lambda i: (i, 0, 0))]
        scratch = [pltpu.VMEM((1, LANES), F32)]
    return pl.pallas_call(
        functools.partial(_proj_kernel, prompt=prompt, tm=tm, seq_len=seq_len),
        grid=(grid,), in_specs=in_specs, out_specs=out_specs, out_shape=out_shape, scratch_shapes=scratch,
        compiler_params=_params(("arbitrary",)), name="proj_prompt" if prompt else "proj_sample",
    )(x, lw["attn_norm"], lw["w_r"], lw["w_q"], lw["w_k"], lw["w_v"], lw["w_f_hi"], lw["w_f_lo"],
      lw["q_norm"], lw["k_norm"], lw["b_forget"])


MLP_FF_CHUNK = 1024


def _mlp_kernel(*refs):
    x_ref, y_refs = refs[0], refs[1:-7]
    wo_ref, g_ref, wu_ref, wd_ref, o_ref, acc_scr, h_scr = refs[-7:]
    c = pl.program_id(1)

    @pl.when(c == 0)
    def _():
        x = x_ref[...]
        off = 0
        for y_ref in y_refs:
            w = y_ref.shape[1]
            x = x + jnp.dot(_bf(y_ref[...]), wo_ref[off:off + w, :], preferred_element_type=F32)
            off += w
        acc_scr[...] = x
        h_scr[...] = _bf(_rms(x, g_ref[...]))

    u = jnp.dot(h_scr[...], wu_ref[...], preferred_element_type=F32)
    u = jnp.square(jnp.maximum(u, 0.0))
    acc_scr[...] += jnp.dot(_bf(u), wd_ref[...], preferred_element_type=F32)

    @pl.when(c == pl.num_programs(1) - 1)
    def _():
        o_ref[...] = acc_scr[...]


def _mlp(x, ys, lw):
    n, d = x.shape
    d_ff = lw["w_up"].shape[1]
    tm = min(512, n)
    fc = MLP_FF_CHUNK
    row = lambda w: pl.BlockSpec((tm, w), lambda i, c: (i, 0))
    return pl.pallas_call(
        _mlp_kernel, grid=(n // tm, d_ff // fc),
        in_specs=([row(d)] + [row(y.shape[1]) for y in ys]
                  + [_const_spec((d, d)), _const_spec((1, d)),
                     pl.BlockSpec((d, fc), lambda i, c: (0, c)), pl.BlockSpec((fc, d), lambda i, c: (c, 0))]),
        out_specs=row(d), out_shape=jax.ShapeDtypeStruct((n, d), F32),
        scratch_shapes=[pltpu.VMEM((tm, d), F32), pltpu.VMEM((tm, d), BF16)],
        compiler_params=_params(("arbitrary", "arbitrary")), name="mlp",
    )(x, *ys, lw["w_out"], lw["mlp_norm"], lw["w_up"], lw["w_down"])


def _rwkv_kernel(p_ref, mu_ref, w0_ref, a0_ref, wwa_ref, gup_ref, kk_ref, ka_ref, rk_ref,
                 gnw_ref, gnb_ref, y_ref, s_scr, carry_scr, *, t_valid, chunk):
    t = pl.program_id(1)
    gsz = p_ref.shape[1]
    gw = GROUP_WIDTH
    p = p_ref[0]
    rows = lax.broadcasted_iota(jnp.int32, (gsz, 1), 0)
    prev = jnp.where(rows == 0, carry_scr[...], pltpu.roll(p, 1, axis=0))
    carry_scr[...] = p[gsz - 1:gsz, :]
    xm = p + (prev - p) * mu_ref[...]
    r, k, v = xm[:, :gw], xm[:, gw:2 * gw], xm[:, 2 * gw:3 * gw]
    xwa = xm[:, 3 * gw:3 * gw + LANES]
    xg = xm[:, 3 * gw + LANES:]
    lane_l = lax.broadcasted_iota(jnp.int32, (1, LANES), 1)
    lora = _mm3(jnp.where(lane_l < LANES // 2, jnp.tanh(xwa), xwa), wwa_ref[...])
    w_log = -_softplus(-(w0_ref[...] + lora[:, :gw])) - RWKV_DECAY_OFFSET
    logw = -jnp.exp(w_log)
    a = _sigmoid(a0_ref[...] + lora[:, gw:])
    g = _mm3(_sigmoid(xg), gup_ref[...])
    seg01 = _seg_ones(gw, HEAD_DIM)
    kk = k * kk_ref[...]
    kk = kk * lax.rsqrt(jnp.maximum(_head_sum(kk * kk, seg01), 1e-24))
    k2 = k * (1.0 + (a - 1.0) * ka_ref[...])
    bonus = _head_sum(r * k2 * rk_ref[...], seg01) * v
    if t_valid is not None:
        ok = (rows + t * gsz) < t_valid
        logw, kk, k2 = jnp.where(ok, logw, 0.0), jnp.where(ok, kk, 0.0), jnp.where(ok, k2, 0.0)
        r, v = jnp.where(ok, r, 0.0), jnp.where(ok, v, 0.0)
    b = kk * a

    ri = lax.broadcasted_iota(jnp.int32, (gsz, gsz), 0)
    ci = lax.broadcasted_iota(jnp.int32, (gsz, gsz), 1)
    same = (ri // chunk) == (ci // chunk)
    strict = same & (ci < ri)
    incl = same & (ci <= ri)
    lc = _mm_fine_lhs(jnp.where(incl, 1.0, 0.0).astype(BF16), logw)
    lcc = _mm_fine_lhs(jnp.where(same, 1.0, 0.0).astype(BF16), logw)
    e_neg = jnp.exp(-lc)
    e_end = jnp.exp(lcc - lc)
    at = -kk * jnp.exp(lc - logw)
    bt, kt = b * e_neg, k2 * e_neg
    rt = r * jnp.exp(lc)
    bd, kd = b * e_end, k2 * e_end
    g_end = jnp.exp(lcc)

    lane = lax.broadcasted_iota(jnp.int32, (1, gw), 1) // HEAD_DIM
    eye = jnp.where(ri == ci, 1.0, 0.0)
    rhs = jnp.concatenate([bt, kt], axis=0)
    n_sq = chunk.bit_length() - 2
    heads = range(N_GROUP_HEADS)
    masks = [lane == h for h in heads]
    grams = [_mm_nt(jnp.concatenate([jnp.where(mh, at, 0.0), jnp.where(mh, rt, 0.0)], axis=0), rhs)
             for mh in masks]
    ab = [jnp.where(strict, gm[:gsz, :gsz], 0.0) for gm in grams]
    ak = [jnp.where(strict, gm[:gsz, gsz:], 0.0) for gm in grams]
    rbs = [jnp.where(incl, gm[gsz:, :gsz], 0.0) for gm in grams]
    rk = [jnp.where(incl, gm[gsz:, gsz:], 0.0) for gm in grams]
    akv = [_mm(jnp.concatenate([ak[h], rk[h]], axis=0), v) for h in heads]
    tinv = [eye + x for x in ab]
    pw = ab
    for _ in range(n_sq):
        pw = [_mm(x, x) for x in pw]
        tinv = [tinv[h] + _mm(tinv[h], pw[h]) for h in heads]
    tw = [_mm(tinv[h], jnp.concatenate([at, akv[h][:gsz]], axis=1)) for h in heads]
    wm = _merge_lanes([x[:, :gw] for x in tw], masks)
    u0 = _merge_lanes([x[:, gw:] for x in tw], masks)
    y0 = _merge_lanes([x[gsz:] for x in akv], masks)
    yield

    r2 = lax.broadcasted_iota(jnp.int32, (gw, gw), 0) // HEAD_DIM
    c2 = lax.broadcasted_iota(jnp.int32, (gw, gw), 1) // HEAD_DIM
    blockdiag = r2 == c2
    s = s_scr[...]
    us, zs = [], []
    for c in range(gsz // chunk):
        sl = slice(c * chunk, (c + 1) * chunk)
        z = _mm_nt(jnp.concatenate([wm[sl], rt[sl]], axis=0), s)
        u = z[:chunk] + u0[sl]
        us.append(u)
        zs.append(z[chunk:])
        upd = _mm_tn(jnp.concatenate([u, v[sl]], axis=0), jnp.concatenate([bd[sl], kd[sl]], axis=0))
        s = jnp.where(blockdiag, s * g_end[c * chunk:c * chunk + 1, :] + upd, 0.0)
        yield
    s_scr[...] = s
    u_all = jnp.concatenate(us, axis=0)
    y = jnp.concatenate(zs, axis=0) + y0 + _merge_lanes([_mm(rbs[h], u_all) for h in heads], masks)

    mean = _head_sum(y, seg01) * (1.0 / HEAD_DIM)
    yc = y - mean
    var = _head_sum(yc * yc, seg01) * (1.0 / HEAD_DIM)
    yn = yc * lax.rsqrt(var + RWKV_GN_EPS) * gnw_ref[...] + gnb_ref[...]
    y_ref[0] = (yn + bonus) * g
    yield s


def _merge_lanes(parts, masks):
    out = parts[-1]
    for x, m in zip(parts[-2::-1], masks[-2::-1]):
        out = jnp.where(m, x, out)
    return out


def _rwkv_multi_kernel(p_ref, sh0_ref, s0_ref, *refs, t_valid, chunk):
    weights, (y_ref, sfin_ref, s_scr, carry_scr) = refs[:-4], refs[-4:]
    t = pl.program_id(1)

    @pl.when(t == 0)
    def _():
        carry_scr[...] = sh0_ref[...]
        s_scr[...] = s0_ref[...]

    gens = [_rwkv_kernel(p_ref.at[pl.ds(i, 1)], *weights, y_ref.at[pl.ds(i, 1)], s_scr.at[i], carry_scr.at[i],
                         t_valid=t_valid, chunk=chunk) for i in range(p_ref.shape[0])]
    states = [None] * len(gens)
    for _ in range(p_ref.shape[1] // chunk + 2):
        for i, gen in enumerate(gens):
            states[i] = next(gen)

    @pl.when(t == pl.num_programs(1) - 1)
    def _():
        for i, s in enumerate(states):
            sfin_ref[i] = s


def _blockdiag_state(s):
    b = s.shape[0]
    eye = jnp.eye(N_GROUP_HEADS, dtype=s.dtype)
    return jnp.einsum("bhij,hg->bhigj", s, eye).reshape(b, GROUP_WIDTH, GROUP_WIDTH)


def _rwkv(p, shift0, s0, lw):
    bsz, t, _ = p.shape
    gsz = RWKV_GROUP
    t_pad = -(-t // gsz) * gsz
    if t_pad != t:
        p = jnp.pad(p, ((0, 0), (0, t_pad - t), (0, 0)))
    gw = GROUP_WIDTH
    vec = lambda w: _const_spec((1, w))
    nb = RWKV_SEQS_PER_STEP if bsz % RWKV_SEQS_PER_STEP == 0 else 1
    y, s_bd = pl.pallas_call(
        functools.partial(_rwkv_multi_kernel, t_valid=None if t_pad == t else t, chunk=RWKV_CHUNK),
        grid=(bsz // nb, t_pad // gsz),
        in_specs=[pl.BlockSpec((nb, gsz, N_RWKV_COLS), lambda b, i: (b, i, 0)),
                  pl.BlockSpec((nb, 1, N_RWKV_COLS), lambda b, i: (b, 0, 0)),
                  pl.BlockSpec((nb, gw, gw), lambda b, i: (b, 0, 0)),
                  vec(N_RWKV_COLS), vec(gw), vec(gw), _const_spec((LANES, 2 * gw)), _const_spec((LANES, gw)),
                  vec(gw), vec(gw), vec(gw), vec(gw), vec(gw)],
        out_specs=[pl.BlockSpec((nb, gsz, gw), lambda b, i: (b, i, 0)),
                   pl.BlockSpec((nb, gw, gw), lambda b, i: (b, 0, 0))],
        out_shape=[jax.ShapeDtypeStruct((bsz, t_pad, gw), F32), jax.ShapeDtypeStruct((bsz, gw, gw), F32)],
        scratch_shapes=[pltpu.VMEM((nb, gw, gw), F32), pltpu.VMEM((nb, 1, N_RWKV_COLS), F32)],
        compiler_params=_params(("arbitrary", "arbitrary")), name="rwkv",
    )(p, shift0.reshape(bsz, 1, N_RWKV_COLS), _blockdiag_state(s0), lw["mu"], lw["w0"], lw["a0"], lw["w_wa"],
      lw["g_up"], lw["k_k"], lw["k_a"], lw["r_k"], lw["gn_w"], lw["gn_b"])
    s_fin = s_bd.reshape(bsz, N_GROUP_HEADS, HEAD_DIM, N_GROUP_HEADS, HEAD_DIM)
    s_fin = jnp.stack([s_fin[:, h, :, h, :] for h in range(N_GROUP_HEADS)], axis=1)
    return y[:, :t], s_fin


ATT_TILE = MOBA_BLOCK
FLASH_TILE = 2 * MOBA_BLOCK


def _head_lane(width=GROUP_WIDTH):
    return lax.broadcasted_iota(jnp.int32, (1, width), 1) // HEAD_DIM


LOG2E = 1.4426950408889634
SB_DEAD = -104.0


def _lane_bcast(col, width=LANES):
    return jnp.broadcast_to(col, (col.shape[0], width))


def _twice(x):
    return jnp.concatenate([x, x], axis=1)


def _merge_heads(parts, low):
    left = jnp.where(low, parts[0][:, :LANES], parts[1][:, :LANES])
    right = jnp.where(low, parts[2][:, LANES:], parts[3][:, LANES:])
    return jnp.concatenate([left, right], axis=1)


def _tri_tables(nq):
    pairs = [(i, j) for i in range(nq) for j in range(i + 1)]
    return (jnp.asarray([p[0] for p in pairs], jnp.int32), jnp.asarray([p[1] for p in pairs], jnp.int32))


def _flash2_kernel(qt_ref, kt_ref, *refs, mode, n_blocks):
    if mode == "fox":
        q_ref, k_ref, v_ref, cq_ref, ck_ref, o_ref, qs_scr, m_scr, l_scr, acc_scr, cq_scr = refs
    else:
        q_ref, k_ref, v_ref, km_ref, o_ref, qs_scr, m_scr, l_scr, acc_scr, sel_scr = refs
    qi = qt_ref[pl.program_id(1)]
    ki = kt_ref[pl.program_id(1)]
    tq, tk = q_ref.shape[0], k_ref.shape[0]
    nh = N_GROUP_HEADS
    lane = _head_lane()
    low = lax.broadcasted_iota(jnp.int32, (1, LANES), 1) < HEAD_DIM

    @pl.when(ki == 0)
    def _():
        m_scr[...] = jnp.full(m_scr.shape, NEG_BIG, F32)
        l_scr[...] = jnp.zeros_like(l_scr)
        acc_scr[...] = jnp.zeros_like(acc_scr)
        q = q_ref[...]
        qs = q * (QK_SCALE * LOG2E)
        for h in range(nh):
            qs_scr[h * tq:(h + 1) * tq, :] = _bf(jnp.where(lane == h, qs, 0.0))
        if mode == "fox":
            cq = cq_ref[...] * LOG2E
            for h in range(nh):
                cq_scr[h * tq:(h + 1) * tq, :] = _lane_bcast(cq[:, h:h + 1])
        else:
            km = km_ref[0]
            nb8 = -(-n_blocks // 8) * 8
            blk = lax.broadcasted_iota(jnp.int32, (nb8, 1), 0)
            own = (qi * tq + lax.broadcasted_iota(jnp.int32, (1, tq), 1)) // MOBA_BLOCK
            for h in range(nh):
                gate = _mm3_nt(km, jnp.where(lane == h, q, 0.0))[:nb8]
                rank = jnp.zeros((nb8, tq), F32)
                for m in range(n_blocks):
                    gm = gate[m:m + 1, :]
                    tie = jnp.where(m < blk, 1.0, 0.0)
                    beats = jnp.where(gm > gate, 1.0, jnp.where(gm == gate, tie, 0.0))
                    rank = rank + beats * jnp.where(m < own, 1.0, 0.0)
                sel = jnp.where(blk < own, jnp.where(rank < MOBA_TOPK, 1.0, 0.0), 0.0)
                sel = jnp.where(blk == own, 1.0, sel)
                sel = jnp.concatenate([sel, jnp.zeros((LANES - nb8, tq), F32)], axis=0)
                sel_scr[h] = sel.T

    n_sub = tk // MOBA_BLOCK

    def step(diag):
        k = _bf(k_ref[...])
        v = _bf(v_ref[...])
        s_all = lax.dot_general(qs_scr[...], k, (((1,), (1,)), ((), ())), preferred_element_type=F32)
        if diag:
            causal = (lax.broadcasted_iota(jnp.int32, (1, tk), 1) <= lax.broadcasted_iota(jnp.int32, (tq, 1), 0))
        probs, alphas = [], []
        for h in range(nh):
            rows = slice(h * tq, (h + 1) * tq)
            s = s_all[rows]
            if mode == "fox":
                s = s - ck_ref[0, h:h + 1, :] * LOG2E
            if diag:
                s = jnp.where(causal, s, NEG_BIG)
            m_old = m_scr[rows]
            if mode == "fox":
                rmax = _lane_bcast(jnp.max(s, axis=1, keepdims=True))
                cq = cq_scr[rows]
                m_new = jnp.maximum(m_old, rmax + cq)
                shifts = [m_new - cq] * n_sub
            else:
                col = lax.broadcasted_iota(jnp.int32, (1, LANES), 1)
                sel = sel_scr[h]
                picked, m_new = [], m_old
                for j in range(n_sub):
                    pk = _lane_bcast(jnp.sum(jnp.where(col == ki * n_sub + j, sel, 0.0), axis=1, keepdims=True)) > 0.5
                    rmax = _lane_bcast(jnp.max(s[:, j * MOBA_BLOCK:(j + 1) * MOBA_BLOCK], axis=1, keepdims=True))
                    m_new = jnp.maximum(m_new, jnp.where(pk, rmax, NEG_BIG))
                    picked.append(pk)
                shifts = [jnp.where(pk, m_new, -NEG_BIG) for pk in picked]
            pr = jnp.exp2(s - jnp.concatenate([_twice(x) for x in shifts], axis=1))
            alpha = jnp.exp2(m_old - m_new)
            l_scr[rows] = alpha * l_scr[rows] + _lane_bcast(jnp.sum(pr, axis=1, keepdims=True))
            m_scr[rows] = m_new
            probs.append(_bf(pr))
            alphas.append(_twice(alpha))
        pv = jnp.dot(jnp.concatenate(probs, axis=0), v, preferred_element_type=F32)
        acc_scr[...] = (acc_scr[...] * _merge_heads(alphas, low)
                        + _merge_heads([pv[h * tq:(h + 1) * tq] for h in range(nh)], low))

    @pl.when(ki < qi)
    def _():
        step(False)

    @pl.when(ki == qi)
    def _():
        step(True)
        inv = _merge_heads([_twice(1.0 / l_scr[h * tq:(h + 1) * tq]) for h in range(nh)], low)
        o_ref[...] = acc_scr[...] * inv


def _flash2(q, kv, *, mode, batch, seq_len, q_col, c=None, c_t=None, kmean=None):
    n = q.shape[0]
    t = FLASH_TILE if seq_len % FLASH_TILE == 0 else ATT_TILE
    nq = seq_len // t
    gw = GROUP_WIDTH
    nh = N_GROUP_HEADS
    v_col = q_col + ATT_WIDTH // gw
    in_specs = [pl.BlockSpec((t, gw), lambda b, s, qt, kt: (b * nq + qt[s], q_col)),
                pl.BlockSpec((t, gw), lambda b, s, qt, kt: (b * nq + kt[s], q_col)),
                pl.BlockSpec((t, gw), lambda b, s, qt, kt: (b * nq + kt[s], v_col))]
    scratch = [pltpu.VMEM((nh * t, gw), BF16), pltpu.VMEM((nh * t, LANES), F32), pltpu.VMEM((nh * t, LANES), F32),
               pltpu.VMEM((t, gw), F32)]
    if mode == "fox":
        in_specs += [pl.BlockSpec((t, nh), lambda b, s, qt, kt: (b * nq + qt[s], 0)),
                     pl.BlockSpec((1, nh, t), lambda b, s, qt, kt: (b, 0, kt[s]))]
        scratch += [pltpu.VMEM((nh * t, LANES), F32)]
        args = (q, kv, kv, c, c_t)
    else:
        in_specs += [pl.BlockSpec((1, LANES, gw), lambda b, s, qt, kt: (b, 0, 0))]
        scratch += [pltpu.VMEM((nh, t, LANES), F32)]
        args = (q, kv, kv, kmean)
    qt, kt = _tri_tables(nq)
    return pl.pallas_call(
        functools.partial(_flash2_kernel, mode=mode, n_blocks=seq_len // MOBA_BLOCK),
        grid_spec=pltpu.PrefetchScalarGridSpec(
            num_scalar_prefetch=2, grid=(batch, qt.shape[0]), in_specs=in_specs,
            out_specs=pl.BlockSpec((t, gw), lambda b, s, qt, kt: (b * nq + qt[s], 0)), scratch_shapes=scratch),
        out_shape=jax.ShapeDtypeStruct((n, gw), F32),
        compiler_params=_params(("arbitrary", "arbitrary")), name="attn_" + mode,
    )(qt, kt, *args)


def _sb2_kernel(qt_ref, st_ref, q_ref, k_ref, v_ref, o_ref, qs_scr, r_scr, acc_scr, dead_ref):
    qi = qt_ref[pl.program_id(1)]
    step_id = st_ref[pl.program_id(1)]
    tq, tk = q_ref.shape[0], k_ref.shape[0]
    nh = N_GROUP_HEADS
    lane = _head_lane()
    low = lax.broadcasted_iota(jnp.int32, (1, LANES), 1) < HEAD_DIM

    @pl.when(step_id == 0)
    def _():
        r_scr[...] = jnp.zeros_like(r_scr)
        acc_scr[...] = jnp.zeros_like(acc_scr)
        dead_ref[0] = 0
        qs = q_ref[...] * QK_SCALE
        for h in range(nh):
            qs_scr[h * tq:(h + 1) * tq, :] = _bf(jnp.where(lane == h, qs, 0.0))

    def step(diag):
        k = _bf(k_ref[...])
        v = _bf(v_ref[...])
        z = lax.dot_general(qs_scr[...], k, (((1,), (1,)), ((), ())), preferred_element_type=F32)
        soft = jnp.maximum(z, 0.0) + jnp.log(1.0 + jnp.exp(-jnp.abs(z)))
        log_beta = z - soft
        keep = -soft
        if diag:
            qrow = lax.broadcasted_iota(jnp.int32, (nh * tq, 1), 0) % tq
            strict = lax.broadcasted_iota(jnp.int32, (1, tk), 1) < qrow
            keep = jnp.where(strict, keep, 0.0)
        later01 = jnp.where(lax.broadcasted_iota(jnp.int32, (tk, tk), 0) > lax.broadcasted_iota(jnp.int32, (tk, tk), 1),
                            1.0, 0.0).astype(BF16)
        r_old = r_scr[...]
        att = jnp.exp(log_beta + _mm_fine_rhs(keep, later01) + _twice(r_old))
        if diag:
            att = jnp.where(strict, att, 0.0)
        pv = jnp.dot(_bf(att), v, preferred_element_type=F32)
        acc_scr[...] += _merge_heads([pv[h * tq:(h + 1) * tq] for h in range(nh)], low)
        r_new = r_old + _lane_bcast(jnp.sum(keep, axis=1, keepdims=True))
        r_scr[...] = r_new
        dead_ref[0] = (jnp.max(r_new) < SB_DEAD).astype(jnp.int32)

    @pl.when(step_id == 0)
    def _():
        step(True)

    @pl.when((step_id > 0) & (dead_ref[0] == 0))
    def _():
        step(False)

    @pl.when(step_id == qi)
    def _():
        o_ref[...] = acc_scr[...]


def _sb2(q, kv, *, batch, seq_len, q_col):
    n = q.shape[0]
    t = ATT_TILE
    nq = seq_len // t
    gw = GROUP_WIDTH
    nh = N_GROUP_HEADS
    v_col = q_col + ATT_WIDTH // gw
    kblk = lambda b, s, qt, st: b * nq + qt[s] - st[s]
    qt, st = _tri_tables(nq)
    return pl.pallas_call(
        _sb2_kernel,
        grid_spec=pltpu.PrefetchScalarGridSpec(
            num_scalar_prefetch=2, grid=(batch, qt.shape[0]),
            in_specs=[pl.BlockSpec((t, gw), lambda b, s, qt, st: (b * nq + qt[s], q_col)),
                      pl.BlockSpec((t, gw), lambda b, s, qt, st: (kblk(b, s, qt, st), q_col)),
                      pl.BlockSpec((t, gw), lambda b, s, qt, st: (kblk(b, s, qt, st), v_col))],
            out_specs=pl.BlockSpec((t, gw), lambda b, s, qt, st: (b * nq + qt[s], 0)),
            scratch_shapes=[pltpu.VMEM((nh * t, gw), BF16), pltpu.VMEM((nh * t, LANES), F32),
                            pltpu.VMEM((t, gw), F32), pltpu.SMEM((1,), jnp.int32)]),
        out_shape=jax.ShapeDtypeStruct((n, gw), F32),
        compiler_params=_params(("arbitrary", "arbitrary")), name="attn_sb",
    )(qt, st, q, kv, kv)


PAGE = 128
LOGF_PAGES_PER_STEP = 16


def _foxpast_kernel(pt_ref, *refs):
    n_grp = LOGF_PAGES_PER_STEP
    page_refs, o_ref, x_scr, carry_scr = refs[:n_grp], refs[n_grp], refs[n_grp + 1], refs[n_grp + 2]
    del pt_ref

    @pl.when(pl.program_id(1) == 0)
    def _():
        carry_scr[...] = jnp.zeros_like(carry_scr)

    for i in range(n_grp):
        x_scr[i:i + 1, :] = page_refs[i][0]
    x = x_scr[...]
    nh = N_GROUP_HEADS
    li = lax.broadcasted_iota(jnp.int32, (PAGE * nh, PAGE), 0)
    ki = lax.broadcasted_iota(jnp.int32, (PAGE * nh, PAGE), 1)
    later01 = jnp.where(lax.broadcasted_iota(jnp.int32, (PAGE, PAGE), 0) > lax.broadcasted_iota(jnp.int32, (PAGE, PAGE), 1),
                        1.0, 0.0).astype(BF16)
    later_pg01 = jnp.where(lax.broadcasted_iota(jnp.int32, (n_grp, n_grp), 1) > lax.broadcasted_iota(jnp.int32, (n_grp, n_grp), 0),
                           1.0, 0.0).astype(BF16)
    heads = range(nh)
    lh = [_mm_exact_rhs(x, jnp.where((li // nh == ki) & (li % nh == h), 1.0, 0.0).astype(BF16))
          for h in heads]
    within = [_mm_exact_rhs(z, later01) for z in lh]
    later_pages = [jnp.sum(_mm_exact_lhs(later_pg01, z), axis=1, keepdims=True) for z in lh]
    total = [jnp.sum(jnp.sum(z, axis=1, keepdims=True), axis=0, keepdims=True) for z in lh]
    carry = carry_scr[...]
    for h in heads:
        o_ref[0, h] = within[h] + later_pages[h] + carry[h:h + 1, :]
    carry_scr[0:nh, :] = carry[0:nh, :] + jnp.concatenate([jnp.broadcast_to(t, (1, PAGE)) for t in total], axis=0)


def _fox_past(page_table, logf_pages, page_offset):
    bsz, n_pages = page_table.shape
    n_grp = LOGF_PAGES_PER_STEP
    groups = n_pages // n_grp

    def page_spec(i):
        return pl.BlockSpec((1, 1, PAGE * N_GROUP_HEADS),
                            lambda b, g, pt: (page_offset + pt[b * n_pages + (groups - 1 - g) * n_grp + i], 0, 0))

    out = pl.pallas_call(
        _foxpast_kernel,
        grid_spec=pltpu.PrefetchScalarGridSpec(
            num_scalar_prefetch=1, grid=(bsz, groups),
            in_specs=[page_spec(i) for i in range(n_grp)],
            out_specs=pl.BlockSpec((1, N_GROUP_HEADS, n_grp, PAGE), lambda b, g, pt: (b, 0, groups - 1 - g, 0)),
            scratch_shapes=[pltpu.VMEM((n_grp, PAGE * N_GROUP_HEADS), F32), pltpu.VMEM((8, PAGE), F32)]),
        out_shape=jax.ShapeDtypeStruct((bsz, N_GROUP_HEADS, n_pages, PAGE), F32),
        compiler_params=_params(("arbitrary", "arbitrary")), name="fox_past",
    )(page_table.reshape(-1), *([logf_pages] * n_grp))
    return out.reshape(bsz, N_GROUP_HEADS, n_pages * PAGE)


N_MIXERS = ATT_WIDTH // GROUP_WIDTH


DECODE_BLOCKS_PER_STEP = 4


def _decode3_kernel(pt_ref, q_ref, kvn_ref, cn_ref, cnt_ref, sfx_ref, *refs, n_blocks):
    del pt_ref
    bps = DECODE_BLOCKS_PER_STEP
    page_refs, o_ref = refs[:2 * bps], refs[2 * bps]
    (qbd_scr, cn_scr, fm_scr, fl_scr, facc_scr, sr_scr, sacc_scr, mg_scr, mm_scr, ml_scr, mo_scr,
     om_scr, ol_scr, oo_scr) = refs[2 * bps + 1:]
    s_id = pl.program_id(1)
    nq = q_ref.shape[0]
    nh = N_GROUP_HEADS
    gw = GROUP_WIDTH
    rows = nh * nq
    lane = lax.broadcasted_iota(jnp.int32, (1, LANES), 1)
    row_head = lax.broadcasted_iota(jnp.int32, (rows, 1), 0) // nq
    qrow = lax.broadcasted_iota(jnp.int32, (rows, 1), 0) % nq
    own_lanes = row_head == _head_lane()

    @pl.when(s_id == 0)
    def _():
        for m in range(N_MIXERS):
            qm = jnp.concatenate([q_ref[:, m * gw:(m + 1) * gw]] * nh, axis=0)
            qbd_scr[m] = jnp.where(own_lanes, qm, 0.0)
        cn = cn_ref[...]
        cn_scr[...] = jnp.concatenate([_lane_bcast(cn[:, h:h + 1]) for h in range(nh)], axis=0)
        fm_scr[...] = jnp.full(fm_scr.shape, NEG_BIG, F32)
        for ref in (fl_scr, facc_scr, sr_scr, sacc_scr, mg_scr, mm_scr, ml_scr):
            ref[...] = jnp.zeros_like(ref)

    def tile_pieces(k_tile, v_tile, fox_rows, mask_incl, mask_strict):
        nk = k_tile.shape[0]
        kb = _bf(k_tile)
        vb = _bf(v_tile)
        dn = (((1,), (1,)), ((), ()))
        z = [lax.dot_general(_bf(qbd_scr[m] * QK_SCALE), kb[:, m * gw:(m + 1) * gw], dn, preferred_element_type=F32)
             for m in range(N_MIXERS)]
        yield
        bias = jnp.concatenate([jnp.broadcast_to(fox_rows[h:h + 1, :], (nq, nk)) for h in range(nh)], axis=0)
        f_sc = z[0] + bias
        m_sc = z[1]
        soft = jnp.maximum(z[2], 0.0) + jnp.log(1.0 + jnp.exp(-jnp.abs(z[2])))
        keep = -soft
        if mask_incl is not None:
            f_sc = jnp.where(mask_incl, f_sc, NEG_BIG)
            m_sc = jnp.where(mask_incl, m_sc, NEG_BIG)
            keep = jnp.where(mask_strict, keep, 0.0)
        f_m = _lane_bcast(jnp.max(f_sc, axis=1, keepdims=True))
        m_m = _lane_bcast(jnp.max(m_sc, axis=1, keepdims=True))
        later01 = jnp.where(lax.broadcasted_iota(jnp.int32, (nk, nk), 0) > lax.broadcasted_iota(jnp.int32, (nk, nk), 1),
                            1.0, 0.0).astype(BF16)
        right = _mm_exact_rhs(keep, later01)
        yield
        f_pr = jnp.exp(f_sc - _lane_bcast(f_m[:, :1], nk))
        m_pr = jnp.exp(m_sc - _lane_bcast(m_m[:, :1], nk))
        att = jnp.exp(z[2] - soft + right)
        if mask_incl is not None:
            att = jnp.where(mask_strict, att, 0.0)
        yield
        out = dict(
            f_m=f_m, f_l=_lane_bcast(jnp.sum(f_pr, axis=1, keepdims=True)),
            f_o=jnp.dot(_bf(f_pr), vb[:, :gw], preferred_element_type=F32),
            m_m=m_m, m_l=_lane_bcast(jnp.sum(m_pr, axis=1, keepdims=True)),
            m_o=jnp.dot(_bf(m_pr), vb[:, gw:2 * gw], preferred_element_type=F32),
            s_keep=_lane_bcast(jnp.sum(keep, axis=1, keepdims=True)),
            s_o=jnp.dot(_bf(att), vb[:, 2 * gw:], preferred_element_type=F32),
            gate=_lane_bcast(jnp.sum(qbd_scr[1] * (jnp.sum(k_tile[:, gw:2 * gw], axis=0, keepdims=True)
                                                   * (1.0 / MOBA_BLOCK)), axis=1, keepdims=True)))
        yield out

    def run_tiles(gens):
        outs = [None] * len(gens)
        for _ in range(4):
            for i, gen in enumerate(gens):
                outs[i] = next(gen)
        return outs

    def merge(pc, block):
        cn = cn_scr[...]
        m_old = fm_scr[...]
        m_tile = pc["f_m"] + cn
        m_new = jnp.maximum(m_old, m_tile)
        a_old = jnp.exp(m_old - m_new)
        a_tile = jnp.exp(m_tile - m_new)
        fm_scr[...] = m_new
        fl_scr[...] = a_old * fl_scr[...] + a_tile * pc["f_l"]
        facc_scr[...] = _twice(a_old) * facc_scr[...] + _twice(a_tile) * pc["f_o"]
        r_old = sr_scr[...]
        sacc_scr[...] = sacc_scr[...] + _twice(jnp.exp(r_old)) * pc["s_o"]
        sr_scr[...] = r_old + pc["s_keep"]
        if block is None:
            om_scr[...], ol_scr[...], oo_scr[...] = pc["m_m"], pc["m_l"], pc["m_o"]
        else:
            here = lane == block
            mg_scr[...] = jnp.where(here, pc["gate"], mg_scr[...])
            mm_scr[...] = jnp.where(here, pc["m_m"], mm_scr[...])
            ml_scr[...] = jnp.where(here, pc["m_l"], ml_scr[...])
            mo_scr[pl.ds(block, 1)] = pc["m_o"][None]

    @pl.when(s_id == 0)
    def _():
        pad = jnp.zeros((PAGE - nq, 2 * ATT_WIDTH), F32)
        kvn = jnp.concatenate([kvn_ref[...], pad], axis=0)
        key = lax.broadcasted_iota(jnp.int32, (1, PAGE), 1)
        (pc,) = run_tiles([tile_pieces(kvn[:, :ATT_WIDTH], kvn[:, ATT_WIDTH:], -cnt_ref[0], key <= qrow, key < qrow)])
        merge(pc, None)

    first = (pl.num_programs(1) - 1 - s_id) * bps
    gens = []
    for i in range(bps - 1, -1, -1):
        p0, p1 = page_refs[2 * i], page_refs[2 * i + 1]
        k_tile = jnp.concatenate([p0[0, :, :ATT_WIDTH], p1[0, :, :ATT_WIDTH]], axis=0)
        v_tile = jnp.concatenate([p0[0, :, ATT_WIDTH:], p1[0, :, ATT_WIDTH:]], axis=0)
        gens.append(tile_pieces(k_tile, v_tile, sfx_ref[0, :, i * MOBA_BLOCK:(i + 1) * MOBA_BLOCK], None, None))
    for j, pc in enumerate(run_tiles(gens)):
        merge(pc, first + bps - 1 - j)

    @pl.when(s_id == pl.num_programs(1) - 1)
    def _():
        def store(m, res):
            out = jnp.zeros((nq, gw), F32)
            for h in range(nh):
                out = jnp.where(_head_lane() == h, res[h * nq:(h + 1) * nq], out)
            o_ref[:, m * gw:(m + 1) * gw] = out

        store(0, facc_scr[...] / _twice(fl_scr[...]))
        store(2, sacc_scr[...])
        gate, m_all, l_all = mg_scr[...], mm_scr[...], ml_scr[...]
        rank = jnp.zeros((rows, LANES), F32)
        for n in range(n_blocks):
            gn = gate[:, n:n + 1]
            tie = jnp.where(n < lane, 1.0, 0.0)
            rank = rank + jnp.where(gn > gate, 1.0, jnp.where(gn == gate, tie, 0.0))
        sel = jnp.where(lane < n_blocks, jnp.where(rank < MOBA_TOPK, 1.0, 0.0), 0.0) > 0.5
        m_own = om_scr[...]
        m_top = jnp.maximum(_lane_bcast(jnp.max(jnp.where(sel, m_all, NEG_BIG), axis=1, keepdims=True)), m_own)
        w = jnp.where(sel, jnp.exp(m_all - m_top), 0.0)
        w_own = jnp.exp(m_own - m_top)
        den = _lane_bcast(jnp.sum(w * l_all, axis=1, keepdims=True)) + w_own * ol_scr[...]
        num = _twice(w_own) * oo_scr[...]
        for n in range(n_blocks):
            num = num + _lane_bcast(w[:, n:n + 1], gw) * mo_scr[n]
        store(1, num / _twice(den))


def _decode3(page_table, q, kv_new, c_new, sfx, kv_pages, page_offset):
    bsz, n_pages = page_table.shape
    nq = q.shape[0] // bsz
    nh = N_GROUP_HEADS
    gw = GROUP_WIDTH
    bps = DECODE_BLOCKS_PER_STEP
    n_blocks = n_pages * PAGE // MOBA_BLOCK
    assert MOBA_BLOCK == 2 * PAGE and n_blocks <= LANES and n_blocks % bps == 0
    n_steps = n_blocks // bps
    cnt = jnp.pad(jnp.swapaxes(c_new.reshape(bsz, nq, nh), 1, 2), ((0, 0), (0, 0), (0, PAGE - nq)))

    def page_spec(r):
        return pl.BlockSpec((1, PAGE, 2 * ATT_WIDTH),
                            lambda b, s, pt: (page_offset + pt[b * n_pages + 2 * bps * (n_steps - 1 - s) + r], 0, 0))

    rows = nh * nq
    stat = lambda: pltpu.VMEM((rows, LANES), F32)
    wide = lambda: pltpu.VMEM((rows, gw), F32)
    return pl.pallas_call(
        functools.partial(_decode3_kernel, n_blocks=n_blocks),
        grid_spec=pltpu.PrefetchScalarGridSpec(
            num_scalar_prefetch=1, grid=(bsz, n_steps),
            in_specs=[pl.BlockSpec((nq, ATT_WIDTH), lambda b, s, pt: (b, 0)),
                      pl.BlockSpec((nq, 2 * ATT_WIDTH), lambda b, s, pt: (b, 0)),
                      pl.BlockSpec((nq, nh), lambda b, s, pt: (b, 0)),
                      pl.BlockSpec((1, nh, PAGE), lambda b, s, pt: (b, 0, 0)),
                      pl.BlockSpec((1, nh, bps * MOBA_BLOCK), lambda b, s, pt: (b, 0, n_steps - 1 - s))]
                     + [page_spec(r) for r in range(2 * bps)],
            out_specs=pl.BlockSpec((nq, ATT_WIDTH), lambda b, s, pt: (b, 0)),
            scratch_shapes=[pltpu.VMEM((N_MIXERS, rows, gw), F32), stat(), stat(), stat(), wide(), stat(), wide(),
                            stat(), stat(), stat(), pltpu.VMEM((n_blocks, rows, gw), F32), stat(), stat(), wide()]),
        out_shape=jax.ShapeDtypeStruct((bsz * nq, ATT_WIDTH), F32),
        compiler_params=_params(("arbitrary", "arbitrary")), name="decode_attn",
    )(page_table.reshape(-1), q, kv_new, c_new, cnt, sfx, *([kv_pages] * (2 * bps)))


def _layer_weights(l, attn_norm, w_in, rwkv_mu, rwkv_w0, rwkv_w_up, rwkv_a0, rwkv_a_up, rwkv_g_up,
                   rwkv_k_k, rwkv_k_a, rwkv_r_k, rwkv_gn_w, rwkv_gn_b, q_norm, k_norm, b_forget,
                   w_out, mlp_norm, w_mlp_up, w_mlp_down):
    w = w_in[l]
    a0, a1, a2, a3 = N_RWKV_COLS, N_RWKV_COLS + ATT_WIDTH, N_RWKV_COLS + 2 * ATT_WIDTH, N_RWKV_COLS + 3 * ATT_WIDTH
    w_f = jnp.pad(w[:, a3:], ((0, 0), (0, LANES - N_GROUP_HEADS)))
    w_f_hi = w_f.astype(BF16)
    rank = rwkv_w_up.shape[1]
    z = jnp.zeros((rank, GROUP_WIDTH), F32)
    w_wa = jnp.concatenate([jnp.concatenate([rwkv_w_up[l], z], axis=1),
                            jnp.concatenate([z, rwkv_a_up[l]], axis=1)], axis=0)
    row = lambda v: v.reshape(1, -1)
    return dict(
        attn_norm=row(attn_norm[l]), w_r=w[:, :a0].astype(BF16), w_q=w[:, a0:a1].astype(BF16),
        w_k=w[:, a1:a2].astype(BF16), w_v=w[:, a2:a3].astype(BF16), w_f_hi=w_f_hi,
        w_f_lo=(w_f - w_f_hi.astype(F32)).astype(BF16), q_norm=row(q_norm[l]), k_norm=row(k_norm[l]),
        b_forget=jnp.pad(row(b_forget[l]), ((0, 0), (0, LANES - N_GROUP_HEADS))),
        mu=row(rwkv_mu[l]), w0=row(rwkv_w0[l]), a0=row(rwkv_a0[l]), w_wa=w_wa, g_up=rwkv_g_up[l],
        k_k=row(rwkv_k_k[l]), k_a=row(rwkv_k_a[l]), r_k=row(rwkv_r_k[l]), gn_w=row(rwkv_gn_w[l]),
        gn_b=row(rwkv_gn_b[l]), w_out=w_out[l].astype(BF16), mlp_norm=row(mlp_norm[l]),
        w_up=w_mlp_up[l].astype(BF16), w_down=w_mlp_down[l].astype(BF16))


def _decoder_layer(x, shift0, s0, lw, past):
    bsz, t, d = x.shape
    n = bsz * t
    xf = x.reshape(n, d)
    prompt = past is None
    outs = _proj(xf, lw, prompt=prompt, seq_len=t)
    p, q, kv, lf, c = outs[:5]
    p3 = p.reshape(bsz, t, N_RWKV_COLS)
    y_rwkv, s_new = _rwkv(p3, shift0, s0, lw)
    if prompt:
        kmean = outs[5].reshape(bsz, t // MOBA_BLOCK, GROUP_WIDTH)
        kmean = jnp.pad(kmean, ((0, 0), (0, LANES - t // MOBA_BLOCK), (0, 0)))
        c_t = jnp.swapaxes(c.reshape(bsz, t, N_GROUP_HEADS), 1, 2)
        ys = [_flash2(q, kv, mode="fox", batch=bsz, seq_len=t, q_col=0, c=c, c_t=c_t),
              _flash2(q, kv, mode="moba", batch=bsz, seq_len=t, q_col=1, kmean=kmean),
              _sb2(q, kv, batch=bsz, seq_len=t, q_col=2)]
    else:
        page_table, kv_pages, logf_pages, page_offset = past
        sfx = _fox_past(page_table, logf_pages, page_offset)
        ys = [_decode3(page_table, q, kv, c, sfx, kv_pages, page_offset)]
    x_new = _mlp(xf, [y_rwkv.reshape(n, GROUP_WIDTH)] + ys, lw)
    kv_new = kv.reshape(bsz, t, 2, ATT_WIDTH // HEAD_DIM, HEAD_DIM)
    return x_new.reshape(bsz, t, d), kv_new, lf.reshape(bsz, t, N_GROUP_HEADS), s_new, p3[:, -1]


def kernel(x_prompt, x_sample, cache_kv, cache_logf, state_wkv, state_shift, page_table, attn_norm, w_in, rwkv_mu,
           rwkv_w0, rwkv_w_up, rwkv_a0, rwkv_a_up, rwkv_g_up, rwkv_k_k, rwkv_k_a, rwkv_r_k, rwkv_gn_w, rwkv_gn_b,
           q_norm, k_norm, b_forget, w_out, mlp_norm, w_mlp_up, w_mlp_down):
    depth, n_pool = cache_kv.shape[:2]
    bp = x_prompt.shape[0]
    kv_pages = cache_kv.reshape(depth * n_pool, PAGE, 2 * ATT_WIDTH)
    logf_pages = cache_logf.reshape(depth * n_pool, 1, PAGE * N_GROUP_HEADS)
    yp, ys = x_prompt, x_sample
    outs = [[] for _ in range(8)]
    for l in range(depth):
        lw = _layer_weights(l, attn_norm, w_in, rwkv_mu, rwkv_w0, rwkv_w_up, rwkv_a0, rwkv_a_up, rwkv_g_up,
                            rwkv_k_k, rwkv_k_a, rwkv_r_k, rwkv_gn_w, rwkv_gn_b, q_norm, k_norm, b_forget,
                            w_out, mlp_norm, w_mlp_up, w_mlp_down)
        shift0 = jnp.zeros((bp, N_RWKV_COLS), yp.dtype)
        s0 = jnp.zeros((bp, N_GROUP_HEADS, HEAD_DIM, HEAD_DIM), yp.dtype)
        yp, kvp, lfp, sp, shp = _decoder_layer(yp, shift0, s0, lw, None)
        past = (page_table, kv_pages, logf_pages, l * n_pool)
        ys, kvs, lfs, ss, shs = _decoder_layer(ys, state_shift[l], state_wkv[l], lw, past)
        for acc, val in zip(outs, (kvp, kvs, lfp, lfs, sp, ss, shp, shs)):
            acc.append(val)
    return (yp, ys) + tuple(jnp.stack(o) for o in outs)
```

```python
import functools

import jax
import jax.numpy as jnp
from jax import lax
from jax.experimental import pallas as pl
from jax.experimental.pallas import tpu as pltpu

F32 = jnp.float32
BF16 = jnp.bfloat16

HEAD_DIM = 64
GROUP_WIDTH = 256
N_GROUP_HEADS = 4
ATT_WIDTH = 3 * GROUP_WIDTH
N_RWKV_COLS = 1024
MOBA_BLOCK = 256
MOBA_TOPK = 3
RMS_EPS = 1e-6
RWKV_GN_EPS = 64e-5
RWKV_DECAY_OFFSET = 0.5
QK_SCALE = HEAD_DIM ** -0.5
NEG_BIG = -1e30

LANES = 128
RWKV_GROUP = 128
RWKV_CHUNK = 32
RWKV_SEQS_PER_STEP = 2
VMEM_LIMIT = 56 * 1024 * 1024


def _bf(x):
    return x.astype(BF16)


def _mm(a, b):
    return jnp.dot(_bf(a), _bf(b), preferred_element_type=F32)


def _mm_nt(a, b):
    return lax.dot_general(_bf(a), _bf(b), (((1,), (1,)), ((), ())), preferred_element_type=F32)


def _mm_tn(a, b):
    return lax.dot_general(_bf(a), _bf(b), (((0,), (0,)), ((), ())), preferred_element_type=F32)


def _split3(x):
    hi = _bf(x)
    r1 = x - hi.astype(F32)
    mid = _bf(r1)
    lo = _bf(r1 - mid.astype(F32))
    return hi, mid, lo


def _mm_exact_rhs(a, b01):
    hi, mid, lo = _split3(a)
    return (jnp.dot(hi, b01, preferred_element_type=F32) + jnp.dot(mid, b01, preferred_element_type=F32)
            + jnp.dot(lo, b01, preferred_element_type=F32))


def _mm_exact_lhs(a01, b):
    hi, mid, lo = _split3(b)
    return (jnp.dot(a01, hi, preferred_element_type=F32) + jnp.dot(a01, mid, preferred_element_type=F32)
            + jnp.dot(a01, lo, preferred_element_type=F32))


def _mm_fine_rhs(a, b01):
    hi = _bf(a)
    mid = _bf(a - hi.astype(F32))
    return jnp.dot(hi, b01, preferred_element_type=F32) + jnp.dot(mid, b01, preferred_element_type=F32)


def _mm_fine_lhs(a01, b):
    hi = _bf(b)
    mid = _bf(b - hi.astype(F32))
    return jnp.dot(a01, hi, preferred_element_type=F32) + jnp.dot(a01, mid, preferred_element_type=F32)


def _mm3(a, b):
    ah = _bf(a)
    al = _bf(a - ah.astype(F32))
    bh = _bf(b)
    bl = _bf(b - bh.astype(F32))
    return (jnp.dot(ah, bh, preferred_element_type=F32) + jnp.dot(ah, bl, preferred_element_type=F32)
            + jnp.dot(al, bh, preferred_element_type=F32))


def _mm3_nt(a, b):
    ah = _bf(a)
    al = _bf(a - ah.astype(F32))
    bh = _bf(b)
    bl = _bf(b - bh.astype(F32))
    dn = (((1,), (1,)), ((), ()))
    return (lax.dot_general(ah, bh, dn, preferred_element_type=F32)
            + lax.dot_general(ah, bl, dn, preferred_element_type=F32)
            + lax.dot_general(al, bh, dn, preferred_element_type=F32))


def _softplus(x):
    return jnp.maximum(x, 0.0) + jnp.log(1.0 + jnp.exp(-jnp.abs(x)))


def _log_sigmoid(x):
    return -_softplus(-x)


def _sigmoid(x):
    return 1.0 / (1.0 + jnp.exp(-x))


def _seg_ones(n, seg):
    r = lax.broadcasted_iota(jnp.int32, (n, n), 0) // seg
    c = lax.broadcasted_iota(jnp.int32, (n, n), 1) // seg
    return jnp.where(r == c, 1.0, 0.0).astype(BF16)


def _head_sum(x, seg01):
    return _mm_fine_rhs(x, seg01)


def _rms(x, g):
    return x * lax.rsqrt(jnp.mean(x * x, axis=-1, keepdims=True) + RMS_EPS) * g


def _const_spec(shape):
    nd = len(shape)
    return pl.BlockSpec(shape, lambda *_: (0,) * nd)


def _params(sem):
    return pltpu.CompilerParams(dimension_semantics=sem, vmem_limit_bytes=VMEM_LIMIT)


def _proj_kernel(*refs, prompt, tm, seq_len):
    if prompt:
        (x_ref, g_ref, wr_ref, wq_ref, wk_ref, wv_ref, wft_ref, qn_ref, kn_ref, bf_ref,
         p_ref, q_ref, kv_ref, lf_ref, c_ref, km_ref, carry_ref) = refs
    else:
        (x_ref, g_ref, wr_ref, wq_ref, wk_ref, wv_ref, wft_ref, qn_ref, kn_ref, bf_ref,
         p_ref, q_ref, kv_ref, lf_ref, c_ref) = refs
    h = _rms(x_ref[...], g_ref[...])
    hb = _bf(h)
    p_ref[...] = jnp.dot(hb, wr_ref[...], preferred_element_type=F32)
    q = jnp.dot(hb, wq_ref[...], preferred_element_type=F32)
    k = jnp.dot(hb, wk_ref[...], preferred_element_type=F32)
    v = jnp.dot(hb, wv_ref[...], preferred_element_type=F32)
    nw = 2 * GROUP_WIDTH
    seg01 = _seg_ones(nw, HEAD_DIM)

    def head_norm(z, gn):
        ms = _head_sum(z * z, seg01) * (1.0 / HEAD_DIM)
        return z * lax.rsqrt(ms + RMS_EPS) * gn

    kn = head_norm(k[:, :nw], kn_ref[...])
    q_ref[:, :nw] = head_norm(q[:, :nw], qn_ref[...])
    q_ref[:, nw:] = q[:, nw:]
    kv_ref[:, :nw] = kn
    kv_ref[:, nw:ATT_WIDTH] = k[:, nw:]
    kv_ref[:, ATT_WIDTH:] = v
    lane_f = lax.broadcasted_iota(jnp.int32, (1, LANES), 1)
    f = jnp.zeros((tm, LANES), F32)
    for j in range(N_GROUP_HEADS):
        f = jnp.where(lane_f == j, jnp.sum(h * wft_ref[j:j + 1, :], axis=1, keepdims=True), f)
    lf = _log_sigmoid(f + bf_ref[...])
    lf_ref[...] = lf[:, :N_GROUP_HEADS]
    r = lax.broadcasted_iota(jnp.int32, (tm, tm), 0)
    c = lax.broadcasted_iota(jnp.int32, (tm, tm), 1)
    if prompt:
        @pl.when(pl.program_id(0) % (seq_len // tm) == 0)
        def _():
            carry_ref[...] = jnp.zeros_like(carry_ref)

        tri01 = jnp.where(c <= r, 1.0, 0.0).astype(BF16)
        cs = _mm_exact_lhs(tri01, lf) + carry_ref[...]
        carry_ref[...] = cs[tm - 1:tm, :]
        km = kn[:, GROUP_WIDTH:nw].reshape(tm // MOBA_BLOCK, MOBA_BLOCK, GROUP_WIDTH)
        km_ref[0] = jnp.sum(km, axis=1) * (1.0 / MOBA_BLOCK)
    else:
        tri01 = jnp.where((c <= r) & (r // seq_len == c // seq_len), 1.0, 0.0).astype(BF16)
        cs = _mm_exact_lhs(tri01, lf)
    c_ref[...] = cs[:, :N_GROUP_HEADS]


def _proj(x, lw, *, prompt, seq_len):
    n, d = x.shape
    tm = 512 if prompt else n
    grid = n // tm
    row = lambda w: pl.BlockSpec((tm, w), lambda i: (i, 0))
    in_specs = [row(d), _const_spec((1, d)), _const_spec((d, N_RWKV_COLS)), _const_spec((d, ATT_WIDTH)),
                _const_spec((d, ATT_WIDTH)), _const_spec((d, ATT_WIDTH)), _const_spec((8, d)),
                _const_spec((1, 2 * GROUP_WIDTH)), _const_spec((1, 2 * GROUP_WIDTH)), _const_spec((1, LANES))]
    out_shape = [jax.ShapeDtypeStruct((n, N_RWKV_COLS), F32), jax.ShapeDtypeStruct((n, ATT_WIDTH), F32),
                 jax.ShapeDtypeStruct((n, 2 * ATT_WIDTH), F32), jax.ShapeDtypeStruct((n, N_GROUP_HEADS), F32),
                 jax.ShapeDtypeStruct((n, N_GROUP_HEADS), F32)]
    out_specs = [row(N_RWKV_COLS), row(ATT_WIDTH), row(2 * ATT_WIDTH), row(N_GROUP_HEADS), row(N_GROUP_HEADS)]
    scratch = []
    if prompt:
        nb = tm // MOBA_BLOCK
        out_shape += [jax.ShapeDtypeStruct((grid, nb, GROUP_WIDTH), F32)]
        out_specs += [pl.BlockSpec((1, nb, GROUP_WIDTH), lambda i: (i, 0, 0))]
        scratch = [pltpu.VMEM((1, LANES), F32)]
    return pl.pallas_call(
        functools.partial(_proj_kernel, prompt=prompt, tm=tm, seq_len=seq_len),
        grid=(grid,), in_specs=in_specs, out_specs=out_specs, out_shape=out_shape, scratch_shapes=scratch,
        compiler_params=_params(("arbitrary",)), name="proj_prompt" if prompt else "proj_sample",
    )(x, lw["attn_norm"], lw["w_r"], lw["w_q"], lw["w_k"], lw["w_v"], lw["w_f_t"],
      lw["q_norm"], lw["k_norm"], lw["b_forget"])


MLP_FF_CHUNK = 1024


def _mlp_kernel(*refs):
    x_ref, y_refs = refs[0], refs[1:-7]
    wo_ref, g_ref, wu_ref, wd_ref, o_ref, acc_scr, h_scr = refs[-7:]
    c = pl.program_id(1)

    @pl.when(c == 0)
    def _():
        x = x_ref[...]
        off = 0
        for y_ref in y_refs:
            w = y_ref.shape[1]
            x = x + jnp.dot(_bf(y_ref[...]), wo_ref[off:off + w, :], preferred_element_type=F32)
            off += w
        acc_scr[...] = x
        h_scr[...] = _bf(_rms(x, g_ref[...]))

    u = jnp.dot(h_scr[...], wu_ref[...], preferred_element_type=F32)
    u = jnp.square(jnp.maximum(u, 0.0))
    acc_scr[...] += jnp.dot(_bf(u), wd_ref[...], preferred_element_type=F32)

    @pl.when(c == pl.num_programs(1) - 1)
    def _():
        o_ref[...] = acc_scr[...]


def _mlp(x, ys, lw):
    n, d = x.shape
    d_ff = lw["w_up"].shape[1]
    tm = min(512, n)
    fc = MLP_FF_CHUNK
    row = lambda w: pl.BlockSpec((tm, w), lambda i, c: (i, 0))
    return pl.pallas_call(
        _mlp_kernel, grid=(n // tm, d_ff // fc),
        in_specs=([row(d)] + [row(y.shape[1]) for y in ys]
                  + [_const_spec((d, d)), _const_spec((1, d)),
                     pl.BlockSpec((d, fc), lambda i, c: (0, c)), pl.BlockSpec((fc, d), lambda i, c: (c, 0))]),
        out_specs=row(d), out_shape=jax.ShapeDtypeStruct((n, d), F32),
        scratch_shapes=[pltpu.VMEM((tm, d), F32), pltpu.VMEM((tm, d), BF16)],
        compiler_params=_params(("arbitrary", "arbitrary")), name="mlp",
    )(x, *ys, lw["w_out"], lw["mlp_norm"], lw["w_up"], lw["w_down"])


def _rwkv_kernel(p_ref, mu_ref, w0_ref, a0_ref, wwa_ref, gup_ref, kk_ref, ka_ref, rk_ref,
                 gnw_ref, gnb_ref, y_ref, s_scr, carry_scr, *, t_valid, chunk):
    t = pl.program_id(1)
    gsz = p_ref.shape[1]
    gw = GROUP_WIDTH
    p = p_ref[0]
    rows = lax.broadcasted_iota(jnp.int32, (gsz, 1), 0)
    prev = jnp.where(rows == 0, carry_scr[...], pltpu.roll(p, 1, axis=0))
    carry_scr[...] = p[gsz - 1:gsz, :]
    xm = p + (prev - p) * mu_ref[...]
    r, k, v = xm[:, :gw], xm[:, gw:2 * gw], xm[:, 2 * gw:3 * gw]
    xwa = xm[:, 3 * gw:3 * gw + LANES]
    xg = xm[:, 3 * gw + LANES:]
    lane_l = lax.broadcasted_iota(jnp.int32, (1, LANES), 1)
    lora = _mm3(jnp.where(lane_l < LANES // 2, jnp.tanh(xwa), xwa), wwa_ref[...])
    w_log = -_softplus(-(w0_ref[...] + lora[:, :gw])) - RWKV_DECAY_OFFSET
    logw = -jnp.exp(w_log)
    a = _sigmoid(a0_ref[...] + lora[:, gw:])
    g = _mm3(_sigmoid(xg), gup_ref[...])
    seg01 = _seg_ones(gw, HEAD_DIM)
    kk = k * kk_ref[...]
    kk = kk * lax.rsqrt(jnp.maximum(_head_sum(kk * kk, seg01), 1e-24))
    k2 = k * (1.0 + (a - 1.0) * ka_ref[...])
    bonus = _head_sum(r * k2 * rk_ref[...], seg01) * v
    if t_valid is not None:
        ok = (rows + t * gsz) < t_valid
        logw, kk, k2 = jnp.where(ok, logw, 0.0), jnp.where(ok, kk, 0.0), jnp.where(ok, k2, 0.0)
        r, v = jnp.where(ok, r, 0.0), jnp.where(ok, v, 0.0)
    b = kk * a

    ri = lax.broadcasted_iota(jnp.int32, (gsz, gsz), 0)
    ci = lax.broadcasted_iota(jnp.int32, (gsz, gsz), 1)
    same = (ri // chunk) == (ci // chunk)
    strict = same & (ci < ri)
    incl = same & (ci <= ri)
    lc = _mm_fine_lhs(jnp.where(incl, 1.0, 0.0).astype(BF16), logw)
    lcc = _mm_fine_lhs(jnp.where(same, 1.0, 0.0).astype(BF16), logw)
    e_neg = jnp.exp(-lc)
    e_end = jnp.exp(lcc - lc)
    at = -kk * jnp.exp(lc - logw)
    bt, kt = b * e_neg, k2 * e_neg
    rt = r * jnp.exp(lc)
    bd, kd = b * e_end, k2 * e_end
    g_end = jnp.exp(lcc)

    lane = lax.broadcasted_iota(jnp.int32, (1, gw), 1) // HEAD_DIM
    eye = jnp.where(ri == ci, 1.0, 0.0)
    rhs = jnp.concatenate([bt, kt], axis=0)
    n_sq = chunk.bit_length() - 2
    heads = range(N_GROUP_HEADS)
    masks = [lane == h for h in heads]
    grams = [_mm_nt(jnp.concatenate([jnp.where(mh, at, 0.0), jnp.where(mh, rt, 0.0)], axis=0), rhs)
             for mh in masks]
    ab = [jnp.where(strict, gm[:gsz, :gsz], 0.0) for gm in grams]
    ak = [jnp.where(strict, gm[:gsz, gsz:], 0.0) for gm in grams]
    rbs = [jnp.where(incl, gm[gsz:, :gsz], 0.0) for gm in grams]
    rk = [jnp.where(incl, gm[gsz:, gsz:], 0.0) for gm in grams]
    akv = [_mm(jnp.concatenate([ak[h], rk[h]], axis=0), v) for h in heads]
    tinv = [eye + x for x in ab]
    pw = ab
    for _ in range(n_sq):
        pw = [_mm(x, x) for x in pw]
        tinv = [tinv[h] + _mm(tinv[h], pw[h]) for h in heads]
    tw = [_mm(tinv[h], jnp.concatenate([at, akv[h][:gsz]], axis=1)) for h in heads]
    wm = _merge_lanes([x[:, :gw] for x in tw], masks)
    u0 = _merge_lanes([x[:, gw:] for x in tw], masks)
    y0 = _merge_lanes([x[gsz:] for x in akv], masks)
    yield

    r2 = lax.broadcasted_iota(jnp.int32, (gw, gw), 0) // HEAD_DIM
    c2 = lax.broadcasted_iota(jnp.int32, (gw, gw), 1) // HEAD_DIM
    blockdiag = r2 == c2
    s = s_scr[...]
    us, zs = [], []
    for c in range(gsz // chunk):
        sl = slice(c * chunk, (c + 1) * chunk)
        z = _mm_nt(jnp.concatenate([wm[sl], rt[sl]], axis=0), s)
        u = z[:chunk] + u0[sl]
        us.append(u)
        zs.append(z[chunk:])
        upd = _mm_tn(jnp.concatenate([u, v[sl]], axis=0), jnp.concatenate([bd[sl], kd[sl]], axis=0))
        s = jnp.where(blockdiag, s * g_end[c * chunk:c * chunk + 1, :] + upd, 0.0)
        yield
    s_scr[...] = s
    u_all = jnp.concatenate(us, axis=0)
    y = jnp.concatenate(zs, axis=0) + y0 + _merge_lanes([_mm(rbs[h], u_all) for h in heads], masks)

    mean = _head_sum(y, seg01) * (1.0 / HEAD_DIM)
    yc = y - mean
    var = _head_sum(yc * yc, seg01) * (1.0 / HEAD_DIM)
    yn = yc * lax.rsqrt(var + RWKV_GN_EPS) * gnw_ref[...] + gnb_ref[...]
    y_ref[0] = (yn + bonus) * g
    yield s


def _merge_lanes(parts, masks):
    out = parts[-1]
    for x, m in zip(parts[-2::-1], masks[-2::-1]):
        out = jnp.where(m, x, out)
    return out


def _rwkv_multi_kernel(p_ref, sh0_ref, s0_ref, *refs, t_valid, chunk):
    weights, (y_ref, sfin_ref, s_scr, carry_scr) = refs[:-4], refs[-4:]
    t = pl.program_id(1)

    @pl.when(t == 0)
    def _():
        carry_scr[...] = sh0_ref[...]
        s_scr[...] = s0_ref[...]

    gens = [_rwkv_kernel(p_ref.at[pl.ds(i, 1)], *weights, y_ref.at[pl.ds(i, 1)], s_scr.at[i], carry_scr.at[i],
                         t_valid=t_valid, chunk=chunk) for i in range(p_ref.shape[0])]
    states = [None] * len(gens)
    for _ in range(p_ref.shape[1] // chunk + 2):
        for i, gen in enumerate(gens):
            states[i] = next(gen)

    @pl.when(t == pl.num_programs(1) - 1)
    def _():
        for i, s in enumerate(states):
            sfin_ref[i] = s


def _blockdiag_state(s):
    b = s.shape[0]
    eye = jnp.eye(N_GROUP_HEADS, dtype=s.dtype)
    return jnp.einsum("bhij,hg->bhigj", s, eye).reshape(b, GROUP_WIDTH, GROUP_WIDTH)


def _rwkv(p, shift0, s0, lw):
    bsz, t, _ = p.shape
    gsz = RWKV_GROUP
    t_pad = -(-t // gsz) * gsz
    if t_pad != t:
        p = jnp.pad(p, ((0, 0), (0, t_pad - t), (0, 0)))
    gw = GROUP_WIDTH
    vec = lambda w: _const_spec((1, w))
    nb = RWKV_SEQS_PER_STEP if bsz % RWKV_SEQS_PER_STEP == 0 else 1
    y, s_bd = pl.pallas_call(
        functools.partial(_rwkv_multi_kernel, t_valid=None if t_pad == t else t, chunk=RWKV_CHUNK),
        grid=(bsz // nb, t_pad // gsz),
        in_specs=[pl.BlockSpec((nb, gsz, N_RWKV_COLS), lambda b, i: (b, i, 0)),
                  pl.BlockSpec((nb, 1, N_RWKV_COLS), lambda b, i: (b, 0, 0)),
                  pl.BlockSpec((nb, gw, gw), lambda b, i: (b, 0, 0)),
                  vec(N_RWKV_COLS), vec(gw), vec(gw), _const_spec((LANES, 2 * gw)), _const_spec((LANES, gw)),
                  vec(gw), vec(gw), vec(gw), vec(gw), vec(gw)],
        out_specs=[pl.BlockSpec((nb, gsz, gw), lambda b, i: (b, i, 0)),
                   pl.BlockSpec((nb, gw, gw), lambda b, i: (b, 0, 0))],
        out_shape=[jax.ShapeDtypeStruct((bsz, t_pad, gw), F32), jax.ShapeDtypeStruct((bsz, gw, gw), F32)],
        scratch_shapes=[pltpu.VMEM((nb, gw, gw), F32), pltpu.VMEM((nb, 1, N_RWKV_COLS), F32)],
        compiler_params=_params(("arbitrary", "arbitrary")), name="rwkv",
    )(p, shift0.reshape(bsz, 1, N_RWKV_COLS), _blockdiag_state(s0), lw["mu"], lw["w0"], lw["a0"], lw["w_wa"],
      lw["g_up"], lw["k_k"], lw["k_a"], lw["r_k"], lw["gn_w"], lw["gn_b"])
    s_fin = s_bd.reshape(bsz, N_GROUP_HEADS, HEAD_DIM, N_GROUP_HEADS, HEAD_DIM)
    s_fin = jnp.stack([s_fin[:, h, :, h, :] for h in range(N_GROUP_HEADS)], axis=1)
    return y[:, :t], s_fin


ATT_TILE = MOBA_BLOCK
FLASH_TILE = 2 * MOBA_BLOCK


def _head_lane(width=GROUP_WIDTH):
    return lax.broadcasted_iota(jnp.int32, (1, width), 1) // HEAD_DIM


LOG2E = 1.4426950408889634
SB_DEAD = -104.0


def _lane_bcast(col, width=LANES):
    return jnp.broadcast_to(col, (col.shape[0], width))


def _twice(x):
    return jnp.concatenate([x, x], axis=1)


def _merge_heads(parts, low):
    left = jnp.where(low, parts[0][:, :LANES], parts[1][:, :LANES])
    right = jnp.where(low, parts[2][:, LANES:], parts[3][:, LANES:])
    return jnp.concatenate([left, right], axis=1)


def _tri_tables(nq):
    pairs = [(i, j) for i in range(nq) for j in range(i + 1)]
    return (jnp.asarray([p[0] for p in pairs], jnp.int32), jnp.asarray([p[1] for p in pairs], jnp.int32))


def _flash2_kernel(qt_ref, kt_ref, *refs, mode, n_blocks):
    if mode == "fox":
        q_ref, k_ref, v_ref, cq_ref, ck_ref, o_ref, qs_scr, m_scr, l_scr, acc_scr, cq_scr = refs
    else:
        q_ref, k_ref, v_ref, km_ref, o_ref, qs_scr, m_scr, l_scr, acc_scr, sel_scr = refs
    qi = qt_ref[pl.program_id(1)]
    ki = kt_ref[pl.program_id(1)]
    tq, tk = q_ref.shape[0], k_ref.shape[0]
    nh = N_GROUP_HEADS
    lane = _head_lane()
    low = lax.broadcasted_iota(jnp.int32, (1, LANES), 1) < HEAD_DIM

    @pl.when(ki == 0)
    def _():
        m_scr[...] = jnp.full(m_scr.shape, NEG_BIG, F32)
        l_scr[...] = jnp.zeros_like(l_scr)
        acc_scr[...] = jnp.zeros_like(acc_scr)
        q = q_ref[...]
        qs = q * (QK_SCALE * LOG2E)
        for h in range(nh):
            qs_scr[h * tq:(h + 1) * tq, :] = _bf(jnp.where(lane == h, qs, 0.0))
        if mode == "fox":
            cq = cq_ref[...] * LOG2E
            for h in range(nh):
                cq_scr[h * tq:(h + 1) * tq, :] = _lane_bcast(cq[:, h:h + 1])
        else:
            km = km_ref[0]
            nb8 = -(-n_blocks // 8) * 8
            blk = lax.broadcasted_iota(jnp.int32, (nb8, 1), 0)
            own = (qi * tq + lax.broadcasted_iota(jnp.int32, (1, tq), 1)) // MOBA_BLOCK
            for h in range(nh):
                gate = _mm3_nt(km, jnp.where(lane == h, q, 0.0))[:nb8]
                rank = jnp.zeros((nb8, tq), F32)
                for m in range(n_blocks):
                    gm = gate[m:m + 1, :]
                    tie = jnp.where(m < blk, 1.0, 0.0)
                    beats = jnp.where(gm > gate, 1.0, jnp.where(gm == gate, tie, 0.0))
                    rank = rank + beats * jnp.where(m < own, 1.0, 0.0)
                sel = jnp.where(blk < own, jnp.where(rank < MOBA_TOPK, 1.0, 0.0), 0.0)
                sel = jnp.where(blk == own, 1.0, sel)
                sel = jnp.concatenate([sel, jnp.zeros((LANES - nb8, tq), F32)], axis=0)
                sel_scr[h] = sel.T

    n_sub = tk // MOBA_BLOCK

    def step(diag):
        k = _bf(k_ref[...])
        v = _bf(v_ref[...])
        s_all = lax.dot_general(qs_scr[...], k, (((1,), (1,)), ((), ())), preferred_element_type=F32)
        if diag:
            causal = (lax.broadcasted_iota(jnp.int32, (1, tk), 1) <= lax.broadcasted_iota(jnp.int32, (tq, 1), 0))
        probs, alphas = [], []
        for h in range(nh):
            rows = slice(h * tq, (h + 1) * tq)
            s = s_all[rows]
            if mode == "fox":
                s = s - ck_ref[0, h:h + 1, :] * LOG2E
            if diag:
                s = jnp.where(causal, s, NEG_BIG)
            m_old = m_scr[rows]
            if mode == "fox":
                rmax = _lane_bcast(jnp.max(s, axis=1, keepdims=True))
                cq = cq_scr[rows]
                m_new = jnp.maximum(m_old, rmax + cq)
                shifts = [m_new - cq] * n_sub
            else:
                col = lax.broadcasted_iota(jnp.int32, (1, LANES), 1)
                sel = sel_scr[h]
                picked, m_new = [], m_old
                for j in range(n_sub):
                    pk = _lane_bcast(jnp.sum(jnp.where(col == ki * n_sub + j, sel, 0.0), axis=1, keepdims=True)) > 0.5
                    rmax = _lane_bcast(jnp.max(s[:, j * MOBA_BLOCK:(j + 1) * MOBA_BLOCK], axis=1, keepdims=True))
                    m_new = jnp.maximum(m_new, jnp.where(pk, rmax, NEG_BIG))
                    picked.append(pk)
                shifts = [jnp.where(pk, m_new, -NEG_BIG) for pk in picked]
            pr = jnp.exp2(s - jnp.concatenate([_twice(x) for x in shifts], axis=1))
            alpha = jnp.exp2(m_old - m_new)
            l_scr[rows] = alpha * l_scr[rows] + _lane_bcast(jnp.sum(pr, axis=1, keepdims=True))
            m_scr[rows] = m_new
            probs.append(_bf(pr))
            alphas.append(_twice(alpha))
        pv = jnp.dot(jnp.concatenate(probs, axis=0), v, preferred_element_type=F32)
        acc_scr[...] = (acc_scr[...] * _merge_heads(alphas, low)
                        + _merge_heads([pv[h * tq:(h + 1) * tq] for h in range(nh)], low))

    @pl.when(ki < qi)
    def _():
        step(False)

    @pl.when(ki == qi)
    def _():
        step(True)
        inv = _merge_heads([_twice(1.0 / l_scr[h * tq:(h + 1) * tq]) for h in range(nh)], low)
        o_ref[...] = acc_scr[...] * inv


def _flash2(q, kv, *, mode, batch, seq_len, q_col, c=None, c_t=None, kmean=None):
    n = q.shape[0]
    t = FLASH_TILE if seq_len % FLASH_TILE == 0 else ATT_TILE
    nq = seq_len // t
    gw = GROUP_WIDTH
    nh = N_GROUP_HEADS
    v_col = q_col + ATT_WIDTH // gw
    in_specs = [pl.BlockSpec((t, gw), lambda b, s, qt, kt: (b * nq + qt[s], q_col)),
                pl.BlockSpec((t, gw), lambda b, s, qt, kt: (b * nq + kt[s], q_col)),
                pl.BlockSpec((t, gw), lambda b, s, qt, kt: (b * nq + kt[s], v_col))]
    scratch = [pltpu.VMEM((nh * t, gw), BF16), pltpu.VMEM((nh * t, LANES), F32), pltpu.VMEM((nh * t, LANES), F32),
               pltpu.VMEM((t, gw), F32)]
    if mode == "fox":
        in_specs += [pl.BlockSpec((t, nh), lambda b, s, qt, kt: (b * nq + qt[s], 0)),
                     pl.BlockSpec((1, nh, t), lambda b, s, qt, kt: (b, 0, kt[s]))]
        scratch += [pltpu.VMEM((nh * t, LANES), F32)]
        args = (q, kv, kv, c, c_t)
    else:
        in_specs += [pl.BlockSpec((1, LANES, gw), lambda b, s, qt, kt: (b, 0, 0))]
        scratch += [pltpu.VMEM((nh, t, LANES), F32)]
        args = (q, kv, kv, kmean)
    qt, kt = _tri_tables(nq)
    return pl.pallas_call(
        functools.partial(_flash2_kernel, mode=mode, n_blocks=seq_len // MOBA_BLOCK),
        grid_spec=pltpu.PrefetchScalarGridSpec(
            num_scalar_prefetch=2, grid=(batch, qt.shape[0]), in_specs=in_specs,
            out_specs=pl.BlockSpec((t, gw), lambda b, s, qt, kt: (b * nq + qt[s], 0)), scratch_shapes=scratch),
        out_shape=jax.ShapeDtypeStruct((n, gw), F32),
        compiler_params=_params(("arbitrary", "arbitrary")), name="attn_" + mode,
    )(qt, kt, *args)


def _sb2_kernel(qt_ref, st_ref, q_ref, k_ref, v_ref, o_ref, qs_scr, r_scr, acc_scr, dead_ref):
    qi = qt_ref[pl.program_id(1)]
    step_id = st_ref[pl.program_id(1)]
    tq, tk = q_ref.shape[0], k_ref.shape[0]
    nh = N_GROUP_HEADS
    lane = _head_lane()
    low = lax.broadcasted_iota(jnp.int32, (1, LANES), 1) < HEAD_DIM

    @pl.when(step_id == 0)
    def _():
        r_scr[...] = jnp.zeros_like(r_scr)
        acc_scr[...] = jnp.zeros_like(acc_scr)
        dead_ref[0] = 0
        qs = q_ref[...] * QK_SCALE
        for h in range(nh):
            qs_scr[h * tq:(h + 1) * tq, :] = _bf(jnp.where(lane == h, qs, 0.0))

    def step(diag):
        k = _bf(k_ref[...])
        v = _bf(v_ref[...])
        z = lax.dot_general(qs_scr[...], k, (((1,), (1,)), ((), ())), preferred_element_type=F32)
        soft = jnp.maximum(z, 0.0) + jnp.log(1.0 + jnp.exp(-jnp.abs(z)))
        log_beta = z - soft
        keep = -soft
        if diag:
            qrow = lax.broadcasted_iota(jnp.int32, (nh * tq, 1), 0) % tq
            strict = lax.broadcasted_iota(jnp.int32, (1, tk), 1) < qrow
            keep = jnp.where(strict, keep, 0.0)
        later01 = jnp.where(lax.broadcasted_iota(jnp.int32, (tk, tk), 0) > lax.broadcasted_iota(jnp.int32, (tk, tk), 1),
                            1.0, 0.0).astype(BF16)
        r_old = r_scr[...]
        att = jnp.exp(log_beta + _mm_fine_rhs(keep, later01) + _twice(r_old))
        if diag:
            att = jnp.where(strict, att, 0.0)
        pv = jnp.dot(_bf(att), v, preferred_element_type=F32)
        acc_scr[...] += _merge_heads([pv[h * tq:(h + 1) * tq] for h in range(nh)], low)
        r_new = r_old + _lane_bcast(jnp.sum(keep, axis=1, keepdims=True))
        r_scr[...] = r_new
        dead_ref[0] = (jnp.max(r_new) < SB_DEAD).astype(jnp.int32)

    @pl.when(step_id == 0)
    def _():
        step(True)

    @pl.when((step_id > 0) & (dead_ref[0] == 0))
    def _():
        step(False)

    @pl.when(step_id == qi)
    def _():
        o_ref[...] = acc_scr[...]


def _sb2(q, kv, *, batch, seq_len, q_col):
    n = q.shape[0]
    t = ATT_TILE
    nq = seq_len // t
    gw = GROUP_WIDTH
    nh = N_GROUP_HEADS
    v_col = q_col + ATT_WIDTH // gw
    kblk = lambda b, s, qt, st: b * nq + qt[s] - st[s]
    qt, st = _tri_tables(nq)
    return pl.pallas_call(
        _sb2_kernel,
        grid_spec=pltpu.PrefetchScalarGridSpec(
            num_scalar_prefetch=2, grid=(batch, qt.shape[0]),
            in_specs=[pl.BlockSpec((t, gw), lambda b, s, qt, st: (b * nq + qt[s], q_col)),
                      pl.BlockSpec((t, gw), lambda b, s, qt, st: (kblk(b, s, qt, st), q_col)),
                      pl.BlockSpec((t, gw), lambda b, s, qt, st: (kblk(b, s, qt, st), v_col))],
            out_specs=pl.BlockSpec((t, gw), lambda b, s, qt, st: (b * nq + qt[s], 0)),
            scratch_shapes=[pltpu.VMEM((nh * t, gw), BF16), pltpu.VMEM((nh * t, LANES), F32),
                            pltpu.VMEM((t, gw), F32), pltpu.SMEM((1,), jnp.int32)]),
        out_shape=jax.ShapeDtypeStruct((n, gw), F32),
        compiler_params=_params(("arbitrary", "arbitrary")), name="attn_sb",
    )(qt, st, q, kv, kv)


PAGE = 128
LOGF_PAGES_PER_STEP = 16


def _foxpast_kernel(pt_ref, *refs):
    n_grp = LOGF_PAGES_PER_STEP
    page_refs, o_ref, x_scr, carry_scr = refs[:n_grp], refs[n_grp], refs[n_grp + 1], refs[n_grp + 2]
    del pt_ref

    @pl.when(pl.program_id(1) == 0)
    def _():
        carry_scr[...] = jnp.zeros_like(carry_scr)

    for i in range(n_grp):
        x_scr[i:i + 1, :] = page_refs[i][0]
    x = x_scr[...]
    nh = N_GROUP_HEADS
    li = lax.broadcasted_iota(jnp.int32, (PAGE * nh, PAGE), 0)
    ki = lax.broadcasted_iota(jnp.int32, (PAGE * nh, PAGE), 1)
    later01 = jnp.where(lax.broadcasted_iota(jnp.int32, (PAGE, PAGE), 0) > lax.broadcasted_iota(jnp.int32, (PAGE, PAGE), 1),
                        1.0, 0.0).astype(BF16)
    later_pg01 = jnp.where(lax.broadcasted_iota(jnp.int32, (n_grp, n_grp), 1) > lax.broadcasted_iota(jnp.int32, (n_grp, n_grp), 0),
                           1.0, 0.0).astype(BF16)
    heads = range(nh)
    lh = [_mm_exact_rhs(x, jnp.where((li // nh == ki) & (li % nh == h), 1.0, 0.0).astype(BF16))
          for h in heads]
    within = [_mm_exact_rhs(z, later01) for z in lh]
    later_pages = [jnp.sum(_mm_exact_lhs(later_pg01, z), axis=1, keepdims=True) for z in lh]
    total = [jnp.sum(jnp.sum(z, axis=1, keepdims=True), axis=0, keepdims=True) for z in lh]
    carry = carry_scr[...]
    for h in heads:
        o_ref[0, h] = within[h] + later_pages[h] + carry[h:h + 1, :]
    carry_scr[0:nh, :] = carry[0:nh, :] + jnp.concatenate([jnp.broadcast_to(t, (1, PAGE)) for t in total], axis=0)


def _fox_past(page_table, logf_pages, page_offset):
    bsz, n_pages = page_table.shape
    n_grp = LOGF_PAGES_PER_STEP
    groups = n_pages // n_grp

    def page_spec(i):
        return pl.BlockSpec((1, 1, PAGE * N_GROUP_HEADS),
                            lambda b, g, pt: (page_offset + pt[b * n_pages + (groups - 1 - g) * n_grp + i], 0, 0))

    out = pl.pallas_call(
        _foxpast_kernel,
        grid_spec=pltpu.PrefetchScalarGridSpec(
            num_scalar_prefetch=1, grid=(bsz, groups),
            in_specs=[page_spec(i) for i in range(n_grp)],
            out_specs=pl.BlockSpec((1, N_GROUP_HEADS, n_grp, PAGE), lambda b, g, pt: (b, 0, groups - 1 - g, 0)),
            scratch_shapes=[pltpu.VMEM((n_grp, PAGE * N_GROUP_HEADS), F32), pltpu.VMEM((8, PAGE), F32)]),
        out_shape=jax.ShapeDtypeStruct((bsz, N_GROUP_HEADS, n_pages, PAGE), F32),
        compiler_params=_params(("arbitrary", "arbitrary")), name="fox_past",
    )(page_table.reshape(-1), *([logf_pages] * n_grp))
    return out.reshape(bsz, N_GROUP_HEADS, n_pages * PAGE)


N_MIXERS = ATT_WIDTH // GROUP_WIDTH


DECODE_BLOCKS_PER_STEP = 8


def _decode3_kernel(pt_ref, q_ref, kvn_ref, cn_ref, cnt_ref, sfx_ref, *refs, n_blocks):
    del pt_ref
    bps = DECODE_BLOCKS_PER_STEP
    page_refs, o_ref = refs[:2 * bps], refs[2 * bps]
    (qbd_scr, cn_scr, fm_scr, fl_scr, facc_scr, sr_scr, sacc_scr, mg_scr, mm_scr, ml_scr, mo_scr,
     om_scr, ol_scr, oo_scr) = refs[2 * bps + 1:]
    s_id = pl.program_id(1)
    nq = q_ref.shape[0]
    nh = N_GROUP_HEADS
    gw = GROUP_WIDTH
    rows = nh * nq
    lane = lax.broadcasted_iota(jnp.int32, (1, LANES), 1)
    row_head = lax.broadcasted_iota(jnp.int32, (rows, 1), 0) // nq
    qrow = lax.broadcasted_iota(jnp.int32, (rows, 1), 0) % nq
    own_lanes = row_head == _head_lane()

    @pl.when(s_id == 0)
    def _():
        for m in range(N_MIXERS):
            qm = jnp.concatenate([q_ref[:, m * gw:(m + 1) * gw]] * nh, axis=0)
            qbd_scr[m] = jnp.where(own_lanes, qm, 0.0)
        cn = cn_ref[...]
        cn_scr[...] = jnp.concatenate([_lane_bcast(cn[:, h:h + 1]) for h in range(nh)], axis=0)
        fm_scr[...] = jnp.full(fm_scr.shape, NEG_BIG, F32)
        for ref in (fl_scr, facc_scr, sr_scr, sacc_scr, mg_scr, mm_scr, ml_scr):
            ref[...] = jnp.zeros_like(ref)

    def tile_pieces(k_tile, v_tile, fox_rows, mask_incl, mask_strict):
        nk = k_tile.shape[0]
        kb = _bf(k_tile)
        vb = _bf(v_tile)
        dn = (((1,), (1,)), ((), ()))
        z = [lax.dot_general(_bf(qbd_scr[m] * QK_SCALE), kb[:, m * gw:(m + 1) * gw], dn, preferred_element_type=F32)
             for m in range(N_MIXERS)]
        yield
        bias = jnp.concatenate([jnp.broadcast_to(fox_rows[h:h + 1, :], (nq, nk)) for h in range(nh)], axis=0)
        f_sc = z[0] + bias
        m_sc = z[1]
        soft = jnp.maximum(z[2], 0.0) + jnp.log(1.0 + jnp.exp(-jnp.abs(z[2])))
        keep = -soft
        if mask_incl is not None:
            f_sc = jnp.where(mask_incl, f_sc, NEG_BIG)
            m_sc = jnp.where(mask_incl, m_sc, NEG_BIG)
            keep = jnp.where(mask_strict, keep, 0.0)
        f_m = _lane_bcast(jnp.max(f_sc, axis=1, keepdims=True))
        m_m = _lane_bcast(jnp.max(m_sc, axis=1, keepdims=True))
        later01 = jnp.where(lax.broadcasted_iota(jnp.int32, (nk, nk), 0) > lax.broadcasted_iota(jnp.int32, (nk, nk), 1),
                            1.0, 0.0).astype(BF16)
        right = _mm_exact_rhs(keep, later01)
        yield
        f_pr = jnp.exp(f_sc - _lane_bcast(f_m[:, :1], nk))
        m_pr = jnp.exp(m_sc - _lane_bcast(m_m[:, :1], nk))
        att = jnp.exp(z[2] - soft + right)
        if mask_incl is not None:
            att = jnp.where(mask_strict, att, 0.0)
        yield
        out = dict(
            f_m=f_m, f_l=_lane_bcast(jnp.sum(f_pr, axis=1, keepdims=True)),
            f_o=jnp.dot(_bf(f_pr), vb[:, :gw], preferred_element_type=F32),
            m_m=m_m, m_l=_lane_bcast(jnp.sum(m_pr, axis=1, keepdims=True)),
            m_o=jnp.dot(_bf(m_pr), vb[:, gw:2 * gw], preferred_element_type=F32),
            s_keep=_lane_bcast(jnp.sum(keep, axis=1, keepdims=True)),
            s_o=jnp.dot(_bf(att), vb[:, 2 * gw:], preferred_element_type=F32),
            gate=_lane_bcast(jnp.sum(qbd_scr[1] * (jnp.sum(k_tile[:, gw:2 * gw], axis=0, keepdims=True)
                                                   * (1.0 / MOBA_BLOCK)), axis=1, keepdims=True)))
        yield out

    def run_tiles(gens):
        outs = [None] * len(gens)
        for _ in range(4):
            for i, gen in enumerate(gens):
                outs[i] = next(gen)
        return outs

    def merge(pc, block):
        cn = cn_scr[...]
        m_old = fm_scr[...]
        m_tile = pc["f_m"] + cn
        m_new = jnp.maximum(m_old, m_tile)
        a_old = jnp.exp(m_old - m_new)
        a_tile = jnp.exp(m_tile - m_new)
        fm_scr[...] = m_new
        fl_scr[...] = a_old * fl_scr[...] + a_tile * pc["f_l"]
        facc_scr[...] = _twice(a_old) * facc_scr[...] + _twice(a_tile) * pc["f_o"]
        r_old = sr_scr[...]
        sacc_scr[...] = sacc_scr[...] + _twice(jnp.exp(r_old)) * pc["s_o"]
        sr_scr[...] = r_old + pc["s_keep"]
        if block is None:
            om_scr[...], ol_scr[...], oo_scr[...] = pc["m_m"], pc["m_l"], pc["m_o"]
        else:
            here = lane == block
            mg_scr[...] = jnp.where(here, pc["gate"], mg_scr[...])
            mm_scr[...] = jnp.where(here, pc["m_m"], mm_scr[...])
            ml_scr[...] = jnp.where(here, pc["m_l"], ml_scr[...])
            mo_scr[pl.ds(block, 1)] = pc["m_o"][None]

    @pl.when(s_id == 0)
    def _():
        pad = jnp.zeros((PAGE - nq, 2 * ATT_WIDTH), F32)
        kvn = jnp.concatenate([kvn_ref[...], pad], axis=0)
        key = lax.broadcasted_iota(jnp.int32, (1, PAGE), 1)
        (pc,) = run_tiles([tile_pieces(kvn[:, :ATT_WIDTH], kvn[:, ATT_WIDTH:], -cnt_ref[0], key <= qrow, key < qrow)])
        merge(pc, None)

    first = (pl.num_programs(1) - 1 - s_id) * bps
    gens = []
    for i in range(bps - 1, -1, -1):
        p0, p1 = page_refs[2 * i], page_refs[2 * i + 1]
        k_tile = jnp.concatenate([p0[0, :, :ATT_WIDTH], p1[0, :, :ATT_WIDTH]], axis=0)
        v_tile = jnp.concatenate([p0[0, :, ATT_WIDTH:], p1[0, :, ATT_WIDTH:]], axis=0)
        gens.append(tile_pieces(k_tile, v_tile, sfx_ref[0, :, i * MOBA_BLOCK:(i + 1) * MOBA_BLOCK], None, None))
    for j, pc in enumerate(run_tiles(gens)):
        merge(pc, first + bps - 1 - j)

    @pl.when(s_id == pl.num_programs(1) - 1)
    def _():
        def store(m, res):
            out = jnp.zeros((nq, gw), F32)
            for h in range(nh):
                out = jnp.where(_head_lane() == h, res[h * nq:(h + 1) * nq], out)
            o_ref[:, m * gw:(m + 1) * gw] = out

        store(0, facc_scr[...] / _twice(fl_scr[...]))
        store(2, sacc_scr[...])
        gate, m_all, l_all = mg_scr[...], mm_scr[...], ml_scr[...]
        rank = jnp.zeros((rows, LANES), F32)
        for n in range(n_blocks):
            gn = gate[:, n:n + 1]
            tie = jnp.where(n < lane, 1.0, 0.0)
            rank = rank + jnp.where(gn > gate, 1.0, jnp.where(gn == gate, tie, 0.0))
        sel = jnp.where(lane < n_blocks, jnp.where(rank < MOBA_TOPK, 1.0, 0.0), 0.0) > 0.5
        m_own = om_scr[...]
        m_top = jnp.maximum(_lane_bcast(jnp.max(jnp.where(sel, m_all, NEG_BIG), axis=1, keepdims=True)), m_own)
        w = jnp.where(sel, jnp.exp(m_all - m_top), 0.0)
        w_own = jnp.exp(m_own - m_top)
        den = _lane_bcast(jnp.sum(w * l_all, axis=1, keepdims=True)) + w_own * ol_scr[...]
        num = _twice(w_own) * oo_scr[...]
        for n in range(n_blocks):
            num = num + _lane_bcast(w[:, n:n + 1], gw) * mo_scr[n]
        store(1, num / _twice(den))


def _decode3(page_table, q, kv_new, c_new, sfx, kv_pages, page_offset):
    bsz, n_pages = page_table.shape
    nq = q.shape[0] // bsz
    nh = N_GROUP_HEADS
    gw = GROUP_WIDTH
    bps = DECODE_BLOCKS_PER_STEP
    n_blocks = n_pages * PAGE // MOBA_BLOCK
    assert MOBA_BLOCK == 2 * PAGE and n_blocks <= LANES and n_blocks % bps == 0
    n_steps = n_blocks // bps
    cnt = jnp.pad(jnp.swapaxes(c_new.reshape(bsz, nq, nh), 1, 2), ((0, 0), (0, 0), (0, PAGE - nq)))

    def page_spec(r):
        return pl.BlockSpec((1, PAGE, 2 * ATT_WIDTH),
                            lambda b, s, pt: (page_offset + pt[b * n_pages + 2 * bps * (n_steps - 1 - s) + r], 0, 0))

    rows = nh * nq
    stat = lambda: pltpu.VMEM((rows, LANES), F32)
    wide = lambda: pltpu.VMEM((rows, gw), F32)
    return pl.pallas_call(
        functools.partial(_decode3_kernel, n_blocks=n_blocks),
        grid_spec=pltpu.PrefetchScalarGridSpec(
            num_scalar_prefetch=1, grid=(bsz, n_steps),
            in_specs=[pl.BlockSpec((nq, ATT_WIDTH), lambda b, s, pt: (b, 0)),
                      pl.BlockSpec((nq, 2 * ATT_WIDTH), lambda b, s, pt: (b, 0)),
                      pl.BlockSpec((nq, nh), lambda b, s, pt: (b, 0)),
                      pl.BlockSpec((1, nh, PAGE), lambda b, s, pt: (b, 0, 0)),
                      pl.BlockSpec((1, nh, bps * MOBA_BLOCK), lambda b, s, pt: (b, 0, n_steps - 1 - s))]
                     + [page_spec(r) for r in range(2 * bps)],
            out_specs=pl.BlockSpec((nq, ATT_WIDTH), lambda b, s, pt: (b, 0)),
            scratch_shapes=[pltpu.VMEM((N_MIXERS, rows, gw), F32), stat(), stat(), stat(), wide(), stat(), wide(),
                            stat(), stat(), stat(), pltpu.VMEM((n_blocks, rows, gw), F32), stat(), stat(), wide()]),
        out_shape=jax.ShapeDtypeStruct((bsz * nq, ATT_WIDTH), F32),
        compiler_params=_params(("arbitrary", "arbitrary")), name="decode_attn",
    )(page_table.reshape(-1), q, kv_new, c_new, cnt, sfx, *([kv_pages] * (2 * bps)))


def _layer_weights(l, attn_norm, w_in, rwkv_mu, rwkv_w0, rwkv_w_up, rwkv_a0, rwkv_a_up, rwkv_g_up,
                   rwkv_k_k, rwkv_k_a, rwkv_r_k, rwkv_gn_w, rwkv_gn_b, q_norm, k_norm, b_forget,
                   w_out, mlp_norm, w_mlp_up, w_mlp_down):
    w = w_in[l]
    a0, a1, a2, a3 = N_RWKV_COLS, N_RWKV_COLS + ATT_WIDTH, N_RWKV_COLS + 2 * ATT_WIDTH, N_RWKV_COLS + 3 * ATT_WIDTH
    w_f_t = jnp.pad(w[:, a3:].T, ((0, 8 - N_GROUP_HEADS), (0, 0)))
    rank = rwkv_w_up.shape[1]
    z = jnp.zeros((rank, GROUP_WIDTH), F32)
    w_wa = jnp.concatenate([jnp.concatenate([rwkv_w_up[l], z], axis=1),
                            jnp.concatenate([z, rwkv_a_up[l]], axis=1)], axis=0)
    row = lambda v: v.reshape(1, -1)
    return dict(
        attn_norm=row(attn_norm[l]), w_r=w[:, :a0].astype(BF16), w_q=w[:, a0:a1].astype(BF16),
        w_k=w[:, a1:a2].astype(BF16), w_v=w[:, a2:a3].astype(BF16), w_f_t=w_f_t,
        q_norm=row(q_norm[l]), k_norm=row(k_norm[l]),
        b_forget=jnp.pad(row(b_forget[l]), ((0, 0), (0, LANES - N_GROUP_HEADS))),
        mu=row(rwkv_mu[l]), w0=row(rwkv_w0[l]), a0=row(rwkv_a0[l]), w_wa=w_wa, g_up=rwkv_g_up[l],
        k_k=row(rwkv_k_k[l]), k_a=row(rwkv_k_a[l]), r_k=row(rwkv_r_k[l]), gn_w=row(rwkv_gn_w[l]),
        gn_b=row(rwkv_gn_b[l]), w_out=w_out[l].astype(BF16), mlp_norm=row(mlp_norm[l]),
        w_up=w_mlp_up[l].astype(BF16), w_down=w_mlp_down[l].astype(BF16))


def _decoder_layer(x, shift0, s0, lw, past):
    bsz, t, d = x.shape
    n = bsz * t
    xf = x.reshape(n, d)
    prompt = past is None
    outs = _proj(xf, lw, prompt=prompt, seq_len=t)
    p, q, kv, lf, c = outs[:5]
    p3 = p.reshape(bsz, t, N_RWKV_COLS)
    y_rwkv, s_new = _rwkv(p3, shift0, s0, lw)
    if prompt:
        kmean = outs[5].reshape(bsz, t // MOBA_BLOCK, GROUP_WIDTH)
        kmean = jnp.pad(kmean, ((0, 0), (0, LANES - t // MOBA_BLOCK), (0, 0)))
        c_t = jnp.swapaxes(c.reshape(bsz, t, N_GROUP_HEADS), 1, 2)
        ys = [_flash2(q, kv, mode="fox", batch=bsz, seq_len=t, q_col=0, c=c, c_t=c_t),
              _flash2(q, kv, mode="moba", batch=bsz, seq_len=t, q_col=1, kmean=kmean),
              _sb2(q, kv, batch=bsz, seq_len=t, q_col=2)]
    else:
        page_table, kv_pages, logf_pages, page_offset = past
        sfx = _fox_past(page_table, logf_pages, page_offset)
        ys = [_decode3(page_table, q, kv, c, sfx, kv_pages, page_offset)]
    x_new = _mlp(xf, [y_rwkv.reshape(n, GROUP_WIDTH)] + ys, lw)
    kv_new = kv.reshape(bsz, t, 2, ATT_WIDTH // HEAD_DIM, HEAD_DIM)
    return x_new.reshape(bsz, t, d), kv_new, lf.reshape(bsz, t, N_GROUP_HEADS), s_new, p3[:, -1]


def kernel(x_prompt, x_sample, cache_kv, cache_logf, state_wkv, state_shift, page_table, attn_norm, w_in, rwkv_mu,
           rwkv_w0, rwkv_w_up, rwkv_a0, rwkv_a_up, rwkv_g_up, rwkv_k_k, rwkv_k_a, rwkv_r_k, rwkv_gn_w, rwkv_gn_b,
           q_norm, k_norm, b_forget, w_out, mlp_norm, w_mlp_up, w_mlp_down):
    depth, n_pool = cache_kv.shape[:2]
    bp = x_prompt.shape[0]
    kv_pages = cache_kv.reshape(depth * n_pool, PAGE, 2 * ATT_WIDTH)
    logf_pages = cache_logf.reshape(depth * n_pool, 1, PAGE * N_GROUP_HEADS)
    yp, ys = x_prompt, x_sample
    outs = [[] for _ in range(8)]
    for l in range(depth):
        lw = _layer_weights(l, attn_norm, w_in, rwkv_mu, rwkv_w0, rwkv_w_up, rwkv_a0, rwkv_a_up, rwkv_g_up,
                            rwkv_k_k, rwkv_k_a, rwkv_r_k, rwkv_gn_w, rwkv_gn_b, q_norm, k_norm, b_forget,
                            w_out, mlp_norm, w_mlp_up, w_mlp_down)
        shift0 = jnp.zeros((bp, N_RWKV_COLS), yp.dtype)
        s0 = jnp.zeros((bp, N_GROUP_HEADS, HEAD_DIM, HEAD_DIM), yp.dtype)
        yp, kvp, lfp, sp, shp = _decoder_layer(yp, shift0, s0, lw, None)
        past = (page_table, kv_pages, logf_pages, l * n_pool)
        ys, kvs, lfs, ss, shs = _decoder_layer(ys, state_shift[l], state_wkv[l], lw, past)
        for acc, val in zip(outs, (kvp, kvs, lfp, lfs, sp, ss, shp, shs)):
            acc.append(val)
    return (yp, ys) + tuple(jnp.stack(o) for o in outs)
```
